```python
import jax
import jax.numpy as jnp
from jax import lax
import numpy as np

D_MODEL = 1024
BATCH = 2
SEQ = 8192
DEPTH = 1

CHUNK = 64
Q_BLOCK = 128
EPS = 1e-6
ROPE_THETA = 10000.0

MLA_HEADS = 8
MLA_Q_RANK = 384
MLA_KV_RANK = 256
MLA_NOPE = 64
MLA_ROPE = 32
MLA_V = 64

GLA_HEADS = 4
GLA_DK = 64
GLA_DV = 128
GLA_GATE_RANK = 16
GLA_TAU = 16.0

N_BRANCH = 2

PEER_HEADS = 8
PEER_NKEYS = 128
PEER_HALF = 128
PEER_TOPK = 16
PEER_BLOCK = 128
PEER_EXPERTS = PEER_NKEYS * PEER_NKEYS

IN_SIZES = (MLA_Q_RANK, MLA_KV_RANK, MLA_ROPE, GLA_HEADS * GLA_DK, GLA_HEADS * GLA_DK, GLA_HEADS * GLA_DV, GLA_GATE_RANK, GLA_HEADS * GLA_DV, N_BRANCH * D_MODEL)
IN_WIDTH = sum(IN_SIZES)

kernel_name = 'hybrid_mla_gla_peer_block'


def _split_points():
    return [int(s) for s in np.cumsum(IN_SIZES)[:-1]]


def rms_norm(x, g):
    xf = x.astype(jnp.float32)
    y = xf * lax.rsqrt(jnp.mean(xf * xf, axis=-1, keepdims=True) + EPS)
    return (y * g.astype(jnp.float32)).astype(x.dtype)


def apply_rope(x, positions):
    half = x.shape[-1] // 2
    inv_freq = ROPE_THETA ** (-jnp.arange(half, dtype=jnp.float32) / half)
    ang = positions.astype(jnp.float32)[:, :, None, None] * inv_freq
    cos, sin = jnp.cos(ang), jnp.sin(ang)
    xf = x.astype(jnp.float32)
    x1, x2 = xf[..., :half], xf[..., half:]
    return jnp.concatenate([x1 * cos - x2 * sin, x2 * cos + x1 * sin], axis=-1).astype(x.dtype)


def mla_attention(q_lat, kv_lat, k_rope, positions, g_q, w_qb, g_kv, w_kvb):
    B, S, _ = q_lat.shape
    H = MLA_HEADS
    dqk = MLA_NOPE + MLA_ROPE
    q = (rms_norm(q_lat, g_q) @ w_qb).reshape(B, S, H, dqk)
    q = jnp.concatenate([q[..., :MLA_NOPE], apply_rope(q[..., MLA_NOPE:], positions)], axis=-1)
    kv = (rms_norm(kv_lat, g_kv) @ w_kvb).reshape(B, S, H, MLA_NOPE + MLA_V)
    k_nope, v = kv[..., :MLA_NOPE], kv[..., MLA_NOPE:]
    k_pe = apply_rope(k_rope[:, :, None, :], positions)
    k = jnp.concatenate([k_nope, jnp.broadcast_to(k_pe, (B, S, H, MLA_ROPE))], axis=-1)
    scale = dqk ** -0.5
    nb = S // Q_BLOCK
    qb = q.reshape(B, nb, Q_BLOCK, H, dqk).transpose(1, 0, 2, 3, 4)
    key_chunk = jnp.arange(S) // CHUNK

    def block(args):
        qi, i = args
        q_chunk = (i * Q_BLOCK + jnp.arange(Q_BLOCK)) // CHUNK
        s = jnp.einsum('bqhd,bkhd->bhqk', qi, k, preferred_element_type=jnp.float32) * scale
        mask = key_chunk[None, :] <= q_chunk[:, None]
        s = jnp.where(mask[None, None], s, -1e30)
        p = jax.nn.softmax(s, axis=-1)
        return jnp.einsum('bhqk,bkhd->bqhd', p.astype(v.dtype), v)

    out = lax.map(block, (qb, jnp.arange(nb)))
    return out.transpose(1, 0, 2, 3, 4).reshape(B, S, H * MLA_V)


def gla_mixer(q, k, v, gate_lr, out_gate, w_a2, b_a2, g_gn):
    B, S, _ = q.shape
    H, dk, dv = GLA_HEADS, GLA_DK, GLA_DV
    nc = S // CHUNK
    f32 = jnp.float32
    qc = q.astype(f32).reshape(B, nc, CHUNK, H, dk) * (dk ** -0.5)
    kc = k.astype(f32).reshape(B, nc, CHUNK, H, dk)
    vc = v.astype(f32).reshape(B, nc, CHUNK, H, dv)
    log_a = jax.nn.log_sigmoid((gate_lr @ w_a2 + b_a2).astype(f32)) / GLA_TAU
    log_a = log_a.reshape(B, nc, CHUNK, H, dk)
    cum = jnp.cumsum(log_a, axis=2)
    cum_last = cum[:, :, -1]
    k_dec = kc * jnp.exp(cum_last[:, :, None] - cum)
    d_state = jnp.einsum('bclhk,bclhv->bchkv', k_dec, vc)
    chunk_decay = jnp.exp(cum_last)

    def step(state, inp):
        dec, ds = inp
        new = dec[..., None] * state + ds
        return new, new

    s0 = jnp.zeros((B, H, dk, dv), f32)
    _, states = lax.scan(step, s0, (chunk_decay.transpose(1, 0, 2, 3), d_state.transpose(1, 0, 2, 3, 4)))
    o = jnp.einsum('bclhk,cbhkv->bclhv', qc, states)
    o = o * lax.rsqrt(jnp.mean(o * o, axis=-1, keepdims=True) + EPS) * g_gn.astype(f32)
    o = o.reshape(B, S, H * dv) * jax.nn.silu(out_gate.astype(f32))
    return o.astype(q.dtype)


def peer_ffn(h, w_q, sub_keys, u_tab, v_tab):
    B, S, D = h.shape
    T = B * S
    PH, K = PEER_HEADS, PEER_TOPK
    hf = h.reshape(T, D)
    q = (hf @ w_q).reshape(T, PH, 2, PEER_HALF)
    scores = jnp.einsum('thpd,hpnd->thpn', q, sub_keys, preferred_element_type=jnp.float32)
    s, idx = lax.top_k(scores, K)
    cand = (s[:, :, 0, :, None] + s[:, :, 1, None, :]).reshape(T, PH, K * K)
    cand_idx = (idx[:, :, 0, :, None] * PEER_NKEYS + idx[:, :, 1, None, :]).reshape(T, PH, K * K)
    best, pos = lax.top_k(cand, K)
    expert = jnp.take_along_axis(cand_idx, pos, axis=-1)
    gate = jax.nn.softmax(best, axis=-1)
    nb = T // PEER_BLOCK

    def block(args):
        xb, eb, gb = args
        u = jnp.take(u_tab, eb, axis=0)
        a = jax.nn.gelu(jnp.einsum('td,thkd->thk', xb, u, preferred_element_type=jnp.float32), approximate=False)
        c = (a * gb).astype(h.dtype)
        return jnp.einsum('thk,thkd->td', c, jnp.take(v_tab, eb, axis=0))

    y = lax.map(block, (hf.reshape(nb, PEER_BLOCK, D), expert.reshape(nb, PEER_BLOCK, PH, K), gate.reshape(nb, PEER_BLOCK, PH, K)))
    return y.reshape(B, S, D)


def setup_inputs(seed: int = 0) -> dict:
    key = jax.random.key(seed)
    ks = jax.random.split(key, 24)
    L, D = DEPTH, D_MODEL
    f32 = jnp.float32

    def nrm(k, shape, fan_in):
        return jax.random.normal(k, shape, f32) * (fan_in ** -0.5)

    def gain(k, shape):
        return 1.0 + 0.01 * jax.random.normal(k, shape, f32)

    x = jax.random.normal(ks[0], (BATCH, SEQ, D), f32)
    offset = jax.random.randint(ks[1], (BATCH, 1), 0, 4096, dtype=jnp.int32)
    positions = offset + jnp.arange(SEQ, dtype=jnp.int32)[None, :]
    return {
        'x': x,
        'positions': positions,
        'g_mix': gain(ks[2], (L, D)),
        'w_in': nrm(ks[3], (L, D, IN_WIDTH), D),
        'g_q_lat': gain(ks[4], (L, MLA_Q_RANK)),
        'w_qb': nrm(ks[5], (L, MLA_Q_RANK, MLA_HEADS * (MLA_NOPE + MLA_ROPE)), MLA_Q_RANK),
        'g_kv_lat': gain(ks[6], (L, MLA_KV_RANK)),
        'w_kvb': nrm(ks[7], (L, MLA_KV_RANK, MLA_HEADS * (MLA_NOPE + MLA_V)), MLA_KV_RANK),
        'w_a2': nrm(ks[8], (L, GLA_GATE_RANK, GLA_HEADS * GLA_DK), GLA_GATE_RANK),
        'b_a2': 0.1 * jax.random.normal(ks[9], (L, GLA_HEADS * GLA_DK), f32),
        'g_gla': gain(ks[10], (L, GLA_HEADS, GLA_DV)),
        'w_branch_a': nrm(ks[11], (L, MLA_HEADS * MLA_V, D), MLA_HEADS * MLA_V),
        'w_branch_b': nrm(ks[12], (L, GLA_HEADS * GLA_DV, D), GLA_HEADS * GLA_DV),
        'w_out': nrm(ks[13], (L, D, D), D),
        'g_ffn': gain(ks[14], (L, D)),
        'w_peer_q': nrm(ks[15], (L, D, PEER_HEADS * 2 * PEER_HALF), D),
        'peer_sub_keys': nrm(ks[16], (L, PEER_HEADS, 2, PEER_NKEYS, PEER_HALF), PEER_HALF),
        'peer_u': nrm(ks[17], (L, PEER_EXPERTS, D), D),
        'peer_v': nrm(ks[18], (L, PEER_EXPERTS, D), PEER_HEADS * PEER_TOPK),
        'g_final': gain(ks[19], (D,)),
    }


def reference(x, positions, g_mix, w_in, g_q_lat, w_qb, g_kv_lat, w_kvb, w_a2, b_a2, g_gla, w_branch_a, w_branch_b, w_out, g_ffn, w_peer_q, peer_sub_keys, peer_u, peer_v, g_final):
    B, S, D = x.shape
    pts = _split_points()
    for l in range(DEPTH):
        h = rms_norm(x, g_mix[l])
        proj = h @ w_in[l]
        q_lat, kv_lat, k_rope, gq, gk, gv, g_lr, g_out, br = jnp.split(proj, pts, axis=-1)
        y_a = mla_attention(q_lat, kv_lat, k_rope, positions, g_q_lat[l], w_qb[l], g_kv_lat[l], w_kvb[l])
        y_b = gla_mixer(gq, gk, gv, g_lr, g_out, w_a2[l], b_a2[l], g_gla[l])
        gates = jax.nn.sigmoid(br.astype(jnp.float32)).reshape(B, S, N_BRANCH, D)
        merged = gates[:, :, 0] * (y_a @ w_branch_a[l]) + gates[:, :, 1] * (y_b @ w_branch_b[l])
        x = x + merged.astype(x.dtype) @ w_out[l]
        x = x + peer_ffn(rms_norm(x, g_ffn[l]), w_peer_q[l], peer_sub_keys[l], peer_u[l], peer_v[l])
    return rms_norm(x, g_final)
```

```python
import functools
import math

import numpy as np
import jax
import jax.numpy as jnp
from jax import lax
from jax.experimental import pallas as pl
from jax.experimental.pallas import tpu as pltpu

F32 = jnp.float32
BF16 = jnp.bfloat16

EPS = 1e-6
ROPE_THETA = 10000.0
CHUNK = 64

MLA_HEADS = 8
MLA_Q_RANK = 384
MLA_KV_RANK = 256
MLA_NOPE = 64
MLA_ROPE = 32
MLA_V = 64
HEAD_LANES = 128

GLA_HEADS = 4
GLA_DK = 64
GLA_DV = 128
GLA_GATE_RANK = 16
GLA_TAU = 16.0

PEER_HEADS = 8
PEER_NKEYS = 128
PEER_HALF = 128
PEER_TOPK = 16

LANES = 128
VMEM_LIMIT_BYTES = 56 * 1024 * 1024

_NT = (((1,), (1,)), ((), ()))


def _dot(a, b):
    return jnp.dot(a, b, preferred_element_type=F32)


def _dot_nt(a, b):
    return lax.dot_general(a, b, _NT, preferred_element_type=F32)


def _rms(x, g):
    return x * lax.rsqrt(jnp.mean(x * x, axis=-1, keepdims=True) + EPS) * g


def _gelu(x):
    return 0.5 * x * (1.0 + lax.erf(x * (2.0 ** -0.5)))


def _params(sem):
    return pltpu.CompilerParams(dimension_semantics=sem, vmem_limit_bytes=VMEM_LIMIT_BYTES)


def _full(shape):
    nd = len(shape)
    return pl.BlockSpec(shape, lambda *_: (0,) * nd)


def _proj_kernel(x_ref, pos_ref, freq_ref, gmix_ref, wlat_ref, wkr_ref, wg_ref, wgt_ref, wbr_ref,
                 gql_ref, wq_ref, wqr_ref, gkv_ref, wk_ref, wv_ref,
                 q_out, k_out, v_out, gq_out, gkt_out, gv_out, glrt_out, gout_out, br_out, *, q_scale):
    h = _rms(x_ref[...], gmix_ref[...]).astype(BF16)
    ang = pos_ref[...] * freq_ref[...]
    cos = jnp.cos(ang)
    sin = jnp.sin(ang)

    lat = _dot(h, wlat_ref[...])
    qn = _rms(lat[:, :MLA_Q_RANK], gql_ref[...]).astype(BF16)
    kvn = _rms(lat[:, MLA_Q_RANK:], gkv_ref[...]).astype(BF16)
    q0 = _dot(qn, wq_ref[...])
    q1 = _dot(qn, wqr_ref[...])
    kr = _dot(h, wkr_ref[...])
    kpe = kr[:, :HEAD_LANES] * cos + kr[:, HEAD_LANES:] * sin
    kk = _dot(kvn, wk_ref[...])
    for hd in range(MLA_HEADS):
        sl = slice(hd * HEAD_LANES, (hd + 1) * HEAD_LANES)
        q_out[:, sl] = ((q0[:, sl] * cos + q1[:, sl] * sin) * q_scale).astype(BF16)
        k_out[:, sl] = (kk[:, sl] + kpe).astype(BF16)
    v_out[...] = _dot(kvn, wv_ref[...]).astype(BF16)

    nqk = GLA_HEADS * GLA_DK
    nv = GLA_HEADS * GLA_DV
    g = _dot(h, wg_ref[...])
    gq_out[...] = (g[:, :nqk] * (GLA_DK ** -0.5)).astype(BF16)
    gv_out[...] = g[:, nqk:nqk + nv].astype(BF16)
    gout_out[...] = g[:, nqk + nv:]
    gt = _dot_nt(wgt_ref[...], h)
    gkt_out[...] = gt[:nqk]
    glrt_out[...] = gt[nqk:].astype(BF16)
    br_out[...] = _dot(h, wbr_ref[...])


def _proj(x2, posf, freq, w, tm):
    T, D = x2.shape
    nqk = GLA_HEADS * GLA_DK
    nv = GLA_HEADS * GLA_DV
    hl = MLA_HEADS * HEAD_LANES
    row = lambda n: pl.BlockSpec((tm, n), lambda i: (i, 0))
    col = lambda n: pl.BlockSpec((n, tm), lambda i: (0, i))
    ins = [x2, posf, freq, w['g_mix'], w['w_lat'], w['w_kr'], w['w_g'], w['w_gt'], w['w_br'],
           w['g_q_lat'], w['w_q'], w['w_qr'], w['g_kv_lat'], w['w_k'], w['w_v']]
    in_specs = [row(D), row(1)] + [_full(a.shape) for a in ins[2:]]
    out_shape = [
        jax.ShapeDtypeStruct((T, hl), BF16), jax.ShapeDtypeStruct((T, hl), BF16),
        jax.ShapeDtypeStruct((T, hl), BF16),
        jax.ShapeDtypeStruct((T, nqk), BF16), jax.ShapeDtypeStruct((nqk, T), F32),
        jax.ShapeDtypeStruct((T, nv), BF16), jax.ShapeDtypeStruct((GLA_GATE_RANK, T), BF16),
        jax.ShapeDtypeStruct((T, nv), F32), jax.ShapeDtypeStruct((T, 2 * D), F32),
    ]
    out_specs = [row(hl), row(hl), row(hl), row(nqk), col(nqk), row(nv), col(GLA_GATE_RANK),
                 row(nv), row(2 * D)]
    q_scale = (MLA_NOPE + MLA_ROPE) ** -0.5 * math.log2(math.e)
    return pl.pallas_call(
        functools.partial(_proj_kernel, q_scale=q_scale),
        grid=(T // tm,), in_specs=in_specs, out_specs=out_specs, out_shape=out_shape,
        compiler_params=_params(("parallel",)), name="proj",
    )(*ins)


def _attn_kernel(q_ref, k_ref, v_ref, o_ref, *, tq):
    i = pl.program_id(2)
    q = q_ref[...]

    def step(j, carry, masked):
        m, l, acc = carry
        start = pl.multiple_of(j * tq, tq)
        kj = k_ref[pl.ds(start, tq), :]
        vj = v_ref[pl.ds(start, tq), :]
        s = _dot_nt(q, kj)
        if masked:
            qc = lax.broadcasted_iota(jnp.int32, (tq, tq), 0) // CHUNK
            kc = lax.broadcasted_iota(jnp.int32, (tq, tq), 1) // CHUNK
            s = jnp.where(kc <= qc, s, -1e30)
        m_new = jnp.maximum(m, jnp.max(s, axis=-1, keepdims=True))
        p = jnp.exp2(s - m_new)
        alpha = jnp.exp2(m - m_new)
        l = alpha * l + jnp.sum(p, axis=-1, keepdims=True)
        acc = alpha * acc + _dot(p.astype(BF16), vj)
        return m_new, l, acc

    init = (jnp.full((tq, 1), -1e30, F32), jnp.zeros((tq, 1), F32), jnp.zeros((tq, HEAD_LANES), F32))
    carry = lax.fori_loop(0, i, lambda j, c: step(j, c, False), init)
    m, l, acc = step(i, carry, True)
    o_ref[...] = (acc / l).astype(BF16)


def _attn(q, k, v, B, S, tq):
    T = q.shape[0]
    nq = S // tq
    qspec = pl.BlockSpec((tq, HEAD_LANES), lambda b, h, i: (b * nq + i, h))
    kvspec = pl.BlockSpec((S, HEAD_LANES), lambda b, h, i: (b, h))
    return pl.pallas_call(
        functools.partial(_attn_kernel, tq=tq),
        grid=(B, MLA_HEADS, nq), in_specs=[qspec, kvspec, kvspec], out_specs=qspec,
        out_shape=jax.ShapeDtypeStruct((T, MLA_HEADS * HEAD_LANES), BF16),
        compiler_params=_params(("parallel", "parallel", "arbitrary")), name="attn",
    )(q, k, v)


def _split3(x):
    hi = x.astype(BF16)
    r = x - hi.astype(F32)
    mid = r.astype(BF16)
    lo = (r - mid.astype(F32)).astype(BF16)
    return hi, mid, lo


def _gla_kernel(gq_ref, gkt_ref, gv_ref, glrt_ref, gout_ref, wat_ref, bcol_ref, ggn_ref, sufm_ref, chm_ref,
                y_ref, st_ref, *, tb):
    @pl.when(pl.program_id(1) == 0)
    def _():
        st_ref[...] = jnp.zeros_like(st_ref)

    z = _dot(wat_ref[...], glrt_ref[...]) + bcol_ref[...]
    la = jax.nn.log_sigmoid(z) * (1.0 / GLA_TAU)
    parts = _split3(la)
    sufm = sufm_ref[...]
    chm = chm_ref[...]
    suf = _dot(parts[0], sufm) + _dot(parts[1], sufm) + _dot(parts[2], sufm)
    tot = _dot(parts[0], chm) + _dot(parts[1], chm) + _dot(parts[2], chm)
    kdec = (gkt_ref[...] * jnp.exp(suf)).astype(BF16)
    dec = jnp.exp(tot)
    gq = gq_ref[...]
    gv = gv_ref[...]
    for c in range(tb // CHUNK):
        fr = slice(c * CHUNK, (c + 1) * CHUNK)
        for hd in range(GLA_HEADS):
            kr = slice(hd * GLA_DK, (hd + 1) * GLA_DK)
            vr = slice(hd * GLA_DV, (hd + 1) * GLA_DV)
            ds = _dot(kdec[kr, fr], gv[fr, vr])
            st = dec[kr, c:c + 1] * st_ref[hd] + ds
            st_ref[hd] = st
            o = _dot(gq[fr, kr], st.astype(BF16))
            o = o * lax.rsqrt(jnp.mean(o * o, axis=-1, keepdims=True) + EPS) * ggn_ref[:, vr]
            y_ref[fr, vr] = (o * jax.nn.silu(gout_ref[fr, vr])).astype(BF16)


def _gla(gq, gkt, gv, glrt, gout, w, B, S, tb):
    T = gq.shape[0]
    nb = S // tb
    nqk = GLA_HEADS * GLA_DK
    nv = GLA_HEADS * GLA_DV
    row = lambda n: pl.BlockSpec((tb, n), lambda b, j: (b * nb + j, 0))
    col = lambda n: pl.BlockSpec((n, tb), lambda b, j: (0, b * nb + j))
    fr = np.arange(tb)
    same = (fr[:, None] // CHUNK) == (fr[None, :] // CHUNK)
    sufm = jnp.asarray(same & (fr[:, None] > fr[None, :]), BF16)
    chm = jnp.asarray((fr[:, None] // CHUNK) == np.arange(LANES)[None, :], BF16)
    consts = [w['w_a2t'], w['b_a2col'], w['g_gla'], sufm, chm]
    return pl.pallas_call(
        functools.partial(_gla_kernel, tb=tb),
        grid=(B, nb),
        in_specs=[row(nqk), col(nqk), row(nv), col(GLA_GATE_RANK), row(nv)] + [_full(a.shape) for a in consts],
        out_specs=row(nv),
        out_shape=jax.ShapeDtypeStruct((T, nv), BF16),
        scratch_shapes=[pltpu.VMEM((GLA_HEADS, GLA_DK, GLA_DV), F32)],
        compiler_params=_params(("parallel", "arbitrary")), name="gla",
    )(gq, gkt, gv, glrt, gout, *consts)


def _merge_kernel(x_ref, ya_ref, yb_ref, br_ref, wa_ref, wb_ref, wo_ref, gffn_ref, wpq_ref, k1_ref, k2_ref, subk2_ref,
                  x1_out, h2_out, s1_out, s2_out, s2h_out):
    D = x_ref.shape[1]
    gates = jax.nn.sigmoid(br_ref[...])
    merged = gates[:, :D] * _dot(ya_ref[...], wa_ref[...]) + gates[:, D:] * _dot(yb_ref[...], wb_ref[...])
    x1 = x_ref[...] + _dot(merged.astype(BF16), wo_ref[...])
    x1_out[...] = x1
    h2 = _rms(x1, gffn_ref[...]).astype(BF16)
    h2_out[...] = h2
    pq = _dot(h2, wpq_ref[...]).astype(BF16)
    nh = PEER_HEADS * PEER_HALF
    s1_out[...] = _dot_nt(k1_ref[...], pq[:, :nh])
    s2_out[...] = _dot_nt(k2_ref[...], pq[:, nh:])
    for hd in range(PEER_HEADS):
        s2h_out[hd] = _dot_nt(subk2_ref[hd], pq[:, nh + hd * PEER_HALF:nh + (hd + 1) * PEER_HALF])


def _merge(x2, ya, yb, br, w, tm):
    T, D = x2.shape
    row = lambda n: pl.BlockSpec((tm, n), lambda i: (i, 0))
    col = lambda n: pl.BlockSpec((n, tm), lambda i: (0, i))
    nkh = PEER_NKEYS * PEER_HEADS
    consts = [w['w_a'], w['w_b'], w['w_out'], w['g_ffn'], w['w_pq'], w['keys1'], w['keys2'], w['subk2']]
    return pl.pallas_call(
        _merge_kernel,
        grid=(T // tm,),
        in_specs=[row(D), row(ya.shape[1]), row(yb.shape[1]), row(2 * D)] + [_full(a.shape) for a in consts],
        out_specs=[row(D), row(D), col(nkh), col(nkh),
                   pl.BlockSpec((PEER_HEADS, PEER_NKEYS, tm), lambda i: (0, 0, i))],
        out_shape=[jax.ShapeDtypeStruct((T, D), F32), jax.ShapeDtypeStruct((T, D), BF16),
                   jax.ShapeDtypeStruct((nkh, T), F32), jax.ShapeDtypeStruct((nkh, T), F32),
                   jax.ShapeDtypeStruct((PEER_HEADS, PEER_NKEYS, T), F32)],
        compiler_params=_params(("parallel",)), name="merge",
    )(x2, ya, yb, br, *consts)


_NSORT = PEER_TOPK + 1
_CAND = [(i, j) for i in range(_NSORT) for j in range(_NSORT) if (i + 1) * (j + 1) <= _NSORT]


def _insert_sorted(ms, x):
    out = []
    for m in ms:
        out.append(jnp.maximum(m, x))
        x = jnp.minimum(m, x)
    return tuple(out)


def _topk_kernel(s1_ref, s2_ref, s2h_ref, tau_out, p1_out, p2_out, thr_ref, top1_ref, top2_ref, zinv_ref, *, tl):
    neg = jnp.full((PEER_HEADS, LANES), -jnp.inf, F32)

    def largest(s_ref, ln):
        body = lambda n, ms: _insert_sorted(ms, s_ref[n, :, ln])
        return lax.fori_loop(0, PEER_NKEYS, body, (neg,) * _NSORT, unroll=4)

    for c0 in range(0, tl, LANES):
        ln = pl.ds(c0, LANES)
        a = largest(s1_ref, ln)
        b = largest(s2_ref, ln)
        cands = [a[i] + b[j] for (i, j) in _CAND]
        best = (neg,) * _NSORT
        for cd in cands:
            best = _insert_sorted(best, cd)
        thr = 0.5 * (best[PEER_TOPK - 1] + best[PEER_TOPK])
        top = a[0] + b[0]
        z = functools.reduce(lambda u, v: u + v, [jnp.where(cd >= thr, jnp.exp(cd - top), 0.0) for cd in cands])
        thr_ref[:, ln] = thr
        top1_ref[:, ln] = a[0]
        top2_ref[:, ln] = b[0]
        zinv_ref[:, ln] = 1.0 / z

    s1 = s1_ref[...]
    tau_out[...] = thr_ref[...][None] - s1
    p1_out[...] = jnp.exp(s1 - top1_ref[...][None])
    for hd in range(PEER_HEADS):
        p2_out[hd] = jnp.exp(s2h_ref[hd] - top2_ref[hd:hd + 1, :]) * zinv_ref[hd:hd + 1, :]


def _topk(s1, s2, s2h, tl):
    T = s2h.shape[2]
    nh_spec = pl.BlockSpec((PEER_NKEYS, PEER_HEADS, tl), lambda i: (0, 0, i))
    hn_spec = pl.BlockSpec((PEER_HEADS, PEER_NKEYS, tl), lambda i: (0, 0, i))
    nh_shape = jax.ShapeDtypeStruct((PEER_NKEYS, PEER_HEADS, T), F32)
    hn_shape = jax.ShapeDtypeStruct((PEER_HEADS, PEER_NKEYS, T), F32)
    return pl.pallas_call(
        functools.partial(_topk_kernel, tl=tl),
        grid=(T // tl,),
        in_specs=[nh_spec, nh_spec, hn_spec],
        out_specs=[nh_spec, nh_spec, hn_spec], out_shape=[nh_shape, nh_shape, hn_shape],
        scratch_shapes=[pltpu.VMEM((PEER_HEADS, tl), F32) for _ in range(4)],
        compiler_params=_params(("parallel",)), name="topk",
    )(s1, s2, s2h)


def _experts_kernel(h2_ref, u_ref, vt_ref, s2h_ref, tau_ref, p1_ref, p2_ref, x1_ref, gfin_ref,
                    out_ref, acc_ref, ct_ref, *, gi):
    j = pl.program_id(1)

    @pl.when(j == 0)
    def _():
        acc_ref[...] = jnp.zeros_like(acc_ref)

    at = _dot_nt(u_ref[...], h2_ref[...])
    for g in range(gi):
        i1 = j * gi + g
        tau = tau_ref[i1]
        p1 = p1_ref[i1]
        wgt = None
        for hd in range(PEER_HEADS):
            sel = s2h_ref[hd] >= tau[hd:hd + 1, :]
            term = jnp.where(sel, p2_ref[hd], 0.0) * p1[hd:hd + 1, :]
            wgt = term if wgt is None else wgt + term
        rows = slice(g * PEER_NKEYS, (g + 1) * PEER_NKEYS)
        ct_ref[rows, :] = (_gelu(at[rows]) * wgt).astype(BF16)
    acc_ref[...] += _dot(vt_ref[...], ct_ref[...])

    @pl.when(j == pl.num_programs(1) - 1)
    def _():
        x = x1_ref[...] + acc_ref[...].T
        out_ref[...] = _rms(x, gfin_ref[...])


def _experts(h2, u, vt, s2h, tau, p1, p2, x1, gfin, tm, te):
    T, D = x1.shape
    E = u.shape[0]
    gi = te // PEER_NKEYS
    tok = lambda n: pl.BlockSpec((tm, n), lambda i, j: (i, 0))
    lanes = lambda a: pl.BlockSpec((a.shape[0], a.shape[1], tm), lambda i, j: (0, 0, i))
    return pl.pallas_call(
        functools.partial(_experts_kernel, gi=gi),
        grid=(T // tm, E // te),
        in_specs=[tok(D), pl.BlockSpec((te, D), lambda i, j: (j, 0)), pl.BlockSpec((D, te), lambda i, j: (0, j)),
                  lanes(s2h), lanes(tau), lanes(p1), lanes(p2), tok(D), _full(gfin.shape)],
        out_specs=tok(D),
        out_shape=jax.ShapeDtypeStruct((T, D), F32),
        scratch_shapes=[pltpu.VMEM((D, tm), F32), pltpu.VMEM((te, tm), BF16)],
        compiler_params=_params(("parallel", "arbitrary")), name="experts",
    )(h2, u, vt, s2h, tau, p1, p2, x1, gfin)


def _rot_half_cols(w):
    half = w.shape[-1] // 2
    return jnp.concatenate([-w[..., half:], w[..., :half]], axis=-1)


def _prep_weights(g_mix, w_in, g_q_lat, w_qb, g_kv_lat, w_kvb, w_a2, b_a2, g_gla, w_branch_a, w_branch_b,
                  w_out, g_ffn, w_peer_q, peer_sub_keys):
    D = w_in.shape[0]
    sizes = (MLA_Q_RANK, MLA_KV_RANK, MLA_ROPE, GLA_HEADS * GLA_DK, GLA_HEADS * GLA_DK, GLA_HEADS * GLA_DV,
             GLA_GATE_RANK, GLA_HEADS * GLA_DV, 2 * D)
    pts = [int(s) for s in np.cumsum(sizes)[:-1]]
    w_ql, w_kvl, w_kr, w_gq, w_gk, w_gv, w_glr, w_go, w_br = jnp.split(w_in, pts, axis=1)
    H = MLA_HEADS
    zpad = lambda *shape: jnp.zeros(shape, F32)
    tail = HEAD_LANES - MLA_NOPE - MLA_ROPE

    def rope_block(wr):
        lead = wr.shape[:-1]
        return jnp.concatenate([zpad(*lead, MLA_NOPE), wr, zpad(*lead, tail)], -1)

    wq3 = w_qb.reshape(MLA_Q_RANK, H, MLA_NOPE + MLA_ROPE)
    q_rope = wq3[..., MLA_NOPE:]
    q0 = jnp.concatenate([wq3[..., :MLA_NOPE], q_rope, zpad(MLA_Q_RANK, H, tail)], -1)
    q1 = rope_block(_rot_half_cols(q_rope))
    wkv3 = w_kvb.reshape(MLA_KV_RANK, H, MLA_NOPE + MLA_V)
    wk = jnp.concatenate([wkv3[..., :MLA_NOPE], zpad(MLA_KV_RANK, H, HEAD_LANES - MLA_NOPE)], -1)
    wv = jnp.concatenate([wkv3[..., MLA_NOPE:], zpad(MLA_KV_RANK, H, HEAD_LANES - MLA_V)], -1)
    wa3 = w_branch_a.reshape(H, MLA_V, D)
    wa = jnp.concatenate([wa3, zpad(H, HEAD_LANES - MLA_V, D)], axis=1).reshape(H * HEAD_LANES, D)

    PH = PEER_HEADS
    wpq = w_peer_q.reshape(D, PH, 2, PEER_HALF).transpose(0, 2, 1, 3).reshape(D, 2 * PH * PEER_HALF)
    eye = jnp.eye(PH, dtype=F32)

    def keys_by_key_head(sk):
        return jnp.einsum('hg,hnd->nhgd', eye, sk).reshape(PEER_NKEYS * PH, PH * PEER_HALF)

    return {
        'g_mix': g_mix.reshape(1, D),
        'w_lat': jnp.concatenate([w_ql, w_kvl], 1).astype(BF16),
        'w_kr': jnp.concatenate([rope_block(w_kr), rope_block(_rot_half_cols(w_kr))], 1).astype(BF16),
        'w_g': jnp.concatenate([w_gq, w_gv, w_go], 1).astype(BF16),
        'w_gt': jnp.concatenate([w_gk, w_glr], 1).T.astype(BF16),
        'w_br': w_br.astype(BF16),
        'g_q_lat': g_q_lat.reshape(1, -1), 'g_kv_lat': g_kv_lat.reshape(1, -1),
        'w_q': q0.reshape(MLA_Q_RANK, H * HEAD_LANES).astype(BF16),
        'w_qr': q1.reshape(MLA_Q_RANK, H * HEAD_LANES).astype(BF16),
        'w_k': wk.reshape(MLA_KV_RANK, H * HEAD_LANES).astype(BF16),
        'w_v': wv.reshape(MLA_KV_RANK, H * HEAD_LANES).astype(BF16),
        'w_a2t': w_a2.T.astype(BF16), 'b_a2col': b_a2.reshape(-1, 1), 'g_gla': g_gla.reshape(1, -1),
        'w_a': wa.astype(BF16), 'w_b': w_branch_b.astype(BF16), 'w_out': w_out.astype(BF16),
        'g_ffn': g_ffn.reshape(1, D), 'w_pq': wpq.astype(BF16),
        'keys1': keys_by_key_head(peer_sub_keys[:, 0]).astype(BF16),
        'keys2': keys_by_key_head(peer_sub_keys[:, 1]).astype(BF16),
        'subk2': peer_sub_keys[:, 1].astype(BF16),
    }


def _rope_freq_lanes():
    half = MLA_ROPE // 2
    inv = ROPE_THETA ** (-np.arange(half, dtype=np.float32) / half)
    lanes = np.zeros((1, HEAD_LANES), np.float32)
    lanes[0, MLA_NOPE:MLA_NOPE + half] = inv
    lanes[0, MLA_NOPE + half:MLA_NOPE + 2 * half] = inv
    return jnp.asarray(lanes)


def _block(x, positions, w, peer_u, peer_v, g_final, *, tm_proj, tq, tb, tm_merge, tl, tm_exp, te):
    B, S, D = x.shape
    T = B * S
    x2 = x.reshape(T, D)
    posf = positions.reshape(T, 1).astype(F32)
    q, k, v, gq, gkt, gv, glrt, gout, br = _proj(x2, posf, _rope_freq_lanes(), w, tm_proj)
    ya = _attn(q, k, v, B, S, tq)
    yb = _gla(gq, gkt, gv, glrt, gout, w, B, S, tb)
    x1, h2, s1, s2, s2h = _merge(x2, ya, yb, br, w, tm_merge)
    s1 = s1.reshape(PEER_NKEYS, PEER_HEADS, T)
    s2 = s2.reshape(PEER_NKEYS, PEER_HEADS, T)
    tau, p1, p2 = _topk(s1, s2, s2h, tl)
    out = _experts(h2, peer_u.astype(BF16), peer_v.T.astype(BF16), s2h, tau, p1, p2, x1,
                   g_final.reshape(1, D), tm_exp, te)
    return out.reshape(B, S, D)


def kernel(x, positions, g_mix, w_in, g_q_lat, w_qb, g_kv_lat, w_kvb, w_a2, b_a2, g_gla, w_branch_a, w_branch_b, w_out, g_ffn, w_peer_q, peer_sub_keys, peer_u, peer_v, g_final):
    assert g_mix.shape[0] == 1, "single-layer block"
    w = _prep_weights(g_mix[0], w_in[0], g_q_lat[0], w_qb[0], g_kv_lat[0], w_kvb[0], w_a2[0], b_a2[0], g_gla[0],
                      w_branch_a[0], w_branch_b[0], w_out[0], g_ffn[0], w_peer_q[0], peer_sub_keys[0])
    return _block(x, positions, w, peer_u[0], peer_v[0], g_final,
                  tm_proj=256, tq=512, tb=512, tm_merge=256, tl=256, tm_exp=512, te=512)
```

```python
import functools
import math

import numpy as np
import jax
import jax.numpy as jnp
from jax import lax
from jax.experimental import pallas as pl
from jax.experimental.pallas import tpu as pltpu

F32 = jnp.float32
BF16 = jnp.bfloat16

EPS = 1e-6
ROPE_THETA = 10000.0
CHUNK = 64

MLA_HEADS = 8
MLA_Q_RANK = 384
MLA_KV_RANK = 256
MLA_NOPE = 64
MLA_ROPE = 32
MLA_V = 64
HEAD_LANES = 128

GLA_HEADS = 4
GLA_DK = 64
GLA_DV = 128
GLA_GATE_RANK = 16
GLA_TAU = 16.0

PEER_HEADS = 8
PEER_NKEYS = 128
PEER_HALF = 128
PEER_TOPK = 16

LANES = 128
VMEM_LIMIT_BYTES = 56 * 1024 * 1024

_NT = (((1,), (1,)), ((), ()))


def _dot(a, b):
    return jnp.dot(a, b, preferred_element_type=F32)


def _dot_nt(a, b):
    return lax.dot_general(a, b, _NT, preferred_element_type=F32)


def _rms(x, g):
    return x * lax.rsqrt(jnp.mean(x * x, axis=-1, keepdims=True) + EPS) * g


def _params(sem):
    return pltpu.CompilerParams(dimension_semantics=sem, vmem_limit_bytes=VMEM_LIMIT_BYTES)


def _full(shape):
    nd = len(shape)
    return pl.BlockSpec(shape, lambda *_: (0,) * nd)


def _proj_kernel(x_ref, pos_ref, freq_ref, gmix_ref, wlat_ref, wkr_ref, wg_ref, wgt_ref, wbr_ref,
                 gql_ref, wq_ref, wqr_ref, gkv_ref, wk_ref, wv_ref,
                 q_out, k_out, v_out, gq_out, gkt_out, gv_out, glrt_out, gout_out, br_out, *, q_scale):
    h = _rms(x_ref[...], gmix_ref[...]).astype(BF16)
    ang = pos_ref[...] * freq_ref[...]
    cos = jnp.cos(ang)
    sin = jnp.sin(ang)

    lat = _dot(h, wlat_ref[...])
    qn = _rms(lat[:, :MLA_Q_RANK], gql_ref[...]).astype(BF16)
    kvn = _rms(lat[:, MLA_Q_RANK:], gkv_ref[...]).astype(BF16)
    q0 = _dot(qn, wq_ref[...])
    q1 = _dot(qn, wqr_ref[...])
    kr = _dot(h, wkr_ref[...])
    kpe = kr[:, :HEAD_LANES] * cos + kr[:, HEAD_LANES:] * sin
    kk = _dot(kvn, wk_ref[...])
    for hd in range(MLA_HEADS):
        sl = slice(hd * HEAD_LANES, (hd + 1) * HEAD_LANES)
        q_out[:, sl] = ((q0[:, sl] * cos + q1[:, sl] * sin) * q_scale).astype(BF16)
        k_out[:, sl] = (kk[:, sl] + kpe).astype(BF16)
    ones_lane = lax.broadcasted_iota(jnp.int32, (1, MLA_HEADS * HEAD_LANES), 1) % HEAD_LANES == MLA_V
    v_out[...] = (_dot(kvn, wv_ref[...]) + ones_lane.astype(F32)).astype(BF16)

    nqk = GLA_HEADS * GLA_DK
    nv = GLA_HEADS * GLA_DV
    g = _dot(h, wg_ref[...])
    gq_out[...] = (g[:, :nqk] * (GLA_DK ** -0.5)).astype(BF16)
    gv_out[...] = g[:, nqk:nqk + nv].astype(BF16)
    gout_out[...] = g[:, nqk + nv:]
    gt = _dot_nt(wgt_ref[...], h)
    gkt_out[...] = gt[:nqk]
    glrt_out[...] = gt[nqk:].astype(BF16)
    br_out[...] = _dot(h, wbr_ref[...])


def _proj(x2, posf, freq, w, tm):
    T, D = x2.shape
    nqk = GLA_HEADS * GLA_DK
    nv = GLA_HEADS * GLA_DV
    hl = MLA_HEADS * HEAD_LANES
    row = lambda n: pl.BlockSpec((tm, n), lambda i: (i, 0))
    col = lambda n: pl.BlockSpec((n, tm), lambda i: (0, i))
    ins = [x2, posf, freq, w['g_mix'], w['w_lat'], w['w_kr'], w['w_g'], w['w_gt'], w['w_br'],
           w['g_q_lat'], w['w_q'], w['w_qr'], w['g_kv_lat'], w['w_k'], w['w_v']]
    in_specs = [row(D), row(1)] + [_full(a.shape) for a in ins[2:]]
    out_shape = [
        jax.ShapeDtypeStruct((T, hl), BF16), jax.ShapeDtypeStruct((T, hl), BF16),
        jax.ShapeDtypeStruct((T, hl), BF16),
        jax.ShapeDtypeStruct((T, nqk), BF16), jax.ShapeDtypeStruct((nqk, T), F32),
        jax.ShapeDtypeStruct((T, nv), BF16), jax.ShapeDtypeStruct((GLA_GATE_RANK, T), BF16),
        jax.ShapeDtypeStruct((T, nv), F32), jax.ShapeDtypeStruct((T, 2 * D), F32),
    ]
    out_specs = [row(hl), row(hl), row(hl), row(nqk), col(nqk), row(nv), col(GLA_GATE_RANK),
                 row(nv), row(2 * D)]
    q_scale = (MLA_NOPE + MLA_ROPE) ** -0.5 * math.log2(math.e)
    return pl.pallas_call(
        functools.partial(_proj_kernel, q_scale=q_scale),
        grid=(T // tm,), in_specs=in_specs, out_specs=out_specs, out_shape=out_shape,
        compiler_params=_params(("parallel",)), name="proj",
    )(*ins)


def _attn_kernel(q_ref, k_ref, v_ref, o_ref, sa_ref, sb_ref, *, tq, nsub):
    i = pl.program_id(2)
    qs = [q_ref[pl.ds(s * tq, tq), :] for s in range(nsub)]

    def rows(ref, j):
        return ref[pl.ds(pl.multiple_of(j * tq, tq), tq), :]

    def scores(j, dst, subs):
        kj = rows(k_ref, j)
        for s in subs:
            dst[s] = _dot_nt(qs[s], kj)

    def update(carry, s, vj, masked):
        m, acc = carry
        if masked:
            qc = lax.broadcasted_iota(jnp.int32, (tq, tq), 0) // CHUNK
            kc = lax.broadcasted_iota(jnp.int32, (tq, tq), 1) // CHUNK
            s = jnp.where(kc <= qc, s, -1e30)
        m_new = jnp.maximum(m, jnp.max(s, axis=-1, keepdims=True))
        p = jnp.exp2(s - m_new)
        acc = jnp.exp2(m - m_new) * acc + _dot(p.astype(BF16), vj)
        return m_new, acc

    def phase(j, carry, cur, nxt):
        scores(j + 1, nxt, range(nsub))
        vj = rows(v_ref, j)
        return tuple(update(carry[s], cur[s], vj, False) for s in range(nsub))

    def body(jj, carry):
        carry = phase(2 * jj, carry, sa_ref, sb_ref)
        return phase(2 * jj + 1, carry, sb_ref, sa_ref)

    init = tuple((jnp.full((tq, 1), -1e30, F32), jnp.zeros((tq, HEAD_LANES), F32)) for _ in range(nsub))
    scores(0, sa_ref, range(nsub))
    carry = list(lax.fori_loop(0, i, body, init))
    scores(2 * i + 1, sb_ref, [1])
    v0 = rows(v_ref, 2 * i)
    carry[0] = update(carry[0], sa_ref[0], v0, True)
    carry[1] = update(carry[1], sa_ref[1], v0, False)
    carry[1] = update(carry[1], sb_ref[1], rows(v_ref, 2 * i + 1), True)
    lane = lax.broadcasted_iota(jnp.int32, (tq, HEAD_LANES), 1)
    for s in range(nsub):
        acc = carry[s][1]
        o = acc / acc[:, MLA_V:MLA_V + 1]
        o_ref[pl.ds(s * tq, tq), :] = jnp.where(lane < MLA_V, o, 0.0).astype(BF16)


def _attn(q, k, v, B, S, tq, nsub):
    assert nsub == 2, "the score double-buffering is written for two query sub-tiles per step"
    T = q.shape[0]
    nq = S // (tq * nsub)
    qspec = pl.BlockSpec((tq * nsub, HEAD_LANES), lambda b, h, i: (b * nq + i, h))
    kvspec = pl.BlockSpec((S, HEAD_LANES), lambda b, h, i: (b, h))
    return pl.pallas_call(
        functools.partial(_attn_kernel, tq=tq, nsub=nsub),
        grid=(B, MLA_HEADS, nq), in_specs=[qspec, kvspec, kvspec], out_specs=qspec,
        out_shape=jax.ShapeDtypeStruct((T, MLA_HEADS * HEAD_LANES), BF16),
        scratch_shapes=[pltpu.VMEM((nsub, tq, tq), F32), pltpu.VMEM((nsub, tq, tq), F32)],
        compiler_params=_params(("parallel", "parallel", "arbitrary")), name="attn",
    )(q, k, v)


def _split3(x):
    hi = x.astype(BF16)
    r = x - hi.astype(F32)
    mid = r.astype(BF16)
    lo = (r - mid.astype(F32)).astype(BF16)
    return hi, mid, lo


def _gla_kernel(gq_ref, gkt_ref, gv_ref, glrt_ref, gout_ref, wat_ref, bcol_ref, ggn_ref, sufm_ref, chm_ref,
                y_ref, st_ref, *, tb):
    @pl.when(pl.program_id(1) == 0)
    def _():
        st_ref[...] = jnp.zeros_like(st_ref)

    z = _dot(wat_ref[...], glrt_ref[...]) + bcol_ref[...]
    la = jax.nn.log_sigmoid(z) * (1.0 / GLA_TAU)
    parts = _split3(la)
    sufm = sufm_ref[...]
    chm = chm_ref[...]
    suf = _dot(parts[0], sufm) + _dot(parts[1], sufm) + _dot(parts[2], sufm)
    tot = _dot(parts[0], chm) + _dot(parts[1], chm) + _dot(parts[2], chm)
    kdec = (gkt_ref[...] * jnp.exp(suf)).astype(BF16)
    dec = jnp.exp(tot)
    gq = gq_ref[...]
    gv = gv_ref[...]
    for c in range(tb // CHUNK):
        fr = slice(c * CHUNK, (c + 1) * CHUNK)
        for hd in range(GLA_HEADS):
            kr = slice(hd * GLA_DK, (hd + 1) * GLA_DK)
            vr = slice(hd * GLA_DV, (hd + 1) * GLA_DV)
            ds = _dot(kdec[kr, fr], gv[fr, vr])
            st = dec[kr, c:c + 1] * st_ref[hd] + ds
            st_ref[hd] = st
            o = _dot(gq[fr, kr], st.astype(BF16))
            o = o * lax.rsqrt(jnp.mean(o * o, axis=-1, keepdims=True) + EPS) * ggn_ref[:, vr]
            y_ref[fr, vr] = (o * jax.nn.silu(gout_ref[fr, vr])).astype(BF16)


def _gla(gq, gkt, gv, glrt, gout, w, B, S, tb):
    T = gq.shape[0]
    nb = S // tb
    nqk = GLA_HEADS * GLA_DK
    nv = GLA_HEADS * GLA_DV
    row = lambda n: pl.BlockSpec((tb, n), lambda b, j: (b * nb + j, 0))
    col = lambda n: pl.BlockSpec((n, tb), lambda b, j: (0, b * nb + j))
    fr = np.arange(tb)
    same = (fr[:, None] // CHUNK) == (fr[None, :] // CHUNK)
    sufm = jnp.asarray(same & (fr[:, None] > fr[None, :]), BF16)
    chm = jnp.asarray((fr[:, None] // CHUNK) == np.arange(LANES)[None, :], BF16)
    consts = [w['w_a2t'], w['b_a2col'], w['g_gla'], sufm, chm]
    return pl.pallas_call(
        functools.partial(_gla_kernel, tb=tb),
        grid=(B, nb),
        in_specs=[row(nqk), col(nqk), row(nv), col(GLA_GATE_RANK), row(nv)] + [_full(a.shape) for a in consts],
        out_specs=row(nv),
        out_shape=jax.ShapeDtypeStruct((T, nv), BF16),
        scratch_shapes=[pltpu.VMEM((GLA_HEADS, GLA_DK, GLA_DV), F32)],
        compiler_params=_params(("parallel", "arbitrary")), name="gla",
    )(gq, gkt, gv, glrt, gout, *consts)


def _merge_kernel(x_ref, ya_ref, yb_ref, br_ref, wa_ref, wb_ref, wo_ref, gffn_ref, wpq_ref, k1_ref, k2_ref, subk2_ref,
                  x1_out, h2_out, s1_out, s2_out, s2h_out):
    D = x_ref.shape[1]
    gates = jax.nn.sigmoid(br_ref[...])
    merged = gates[:, :D] * _dot(ya_ref[...], wa_ref[...]) + gates[:, D:] * _dot(yb_ref[...], wb_ref[...])
    x1 = x_ref[...] + _dot(merged.astype(BF16), wo_ref[...])
    x1_out[...] = x1
    h2 = _rms(x1, gffn_ref[...]).astype(BF16)
    h2_out[...] = h2
    pq = _dot(h2, wpq_ref[...]).astype(BF16)
    nh = PEER_HEADS * PEER_HALF
    s1_out[...] = _dot_nt(k1_ref[...], pq[:, :nh])
    s2_out[...] = _dot_nt(k2_ref[...], pq[:, nh:])
    for hd in range(PEER_HEADS):
        s2h_out[hd] = _dot_nt(subk2_ref[hd], pq[:, nh + hd * PEER_HALF:nh + (hd + 1) * PEER_HALF])


def _merge(x2, ya, yb, br, w, tm):
    T, D = x2.shape
    row = lambda n: pl.BlockSpec((tm, n), lambda i: (i, 0))
    col = lambda n: pl.BlockSpec((n, tm), lambda i: (0, i))
    nkh = PEER_NKEYS * PEER_HEADS
    consts = [w['w_a'], w['w_b'], w['w_out'], w['g_ffn'], w['w_pq'], w['keys1'], w['keys2'], w['subk2']]
    return pl.pallas_call(
        _merge_kernel,
        grid=(T // tm,),
        in_specs=[row(D), row(ya.shape[1]), row(yb.shape[1]), row(2 * D)] + [_full(a.shape) for a in consts],
        out_specs=[row(D), row(D), col(nkh), col(nkh),
                   pl.BlockSpec((PEER_HEADS, PEER_NKEYS, tm), lambda i: (0, 0, i))],
        out_shape=[jax.ShapeDtypeStruct((T, D), F32), jax.ShapeDtypeStruct((T, D), BF16),
                   jax.ShapeDtypeStruct((nkh, T), F32), jax.ShapeDtypeStruct((nkh, T), F32),
                   jax.ShapeDtypeStruct((PEER_HEADS, PEER_NKEYS, T), F32)],
        compiler_params=_params(("parallel",)), name="merge",
    )(x2, ya, yb, br, *consts)


_NSORT = PEER_TOPK + 1
_CAND = [(i, j) for i in range(_NSORT) for j in range(_NSORT) if (i + 1) * (j + 1) <= _NSORT]


def _insert_sorted(ms, x):
    out = []
    for m in ms:
        out.append(jnp.maximum(m, x))
        x = jnp.minimum(m, x)
    return tuple(out)


def _topk_kernel(s1_ref, s2_ref, s2h_ref, tau_out, p1_out, p2_out, thr_ref, top1_ref, top2_ref, zinv_ref, *, tl):
    neg = jnp.full((PEER_HEADS, LANES), -jnp.inf, F32)

    def largest(s_ref, ln):
        body = lambda n, ms: _insert_sorted(ms, s_ref[n, :, ln])
        return lax.fori_loop(0, PEER_NKEYS, body, (neg,) * _NSORT, unroll=4)

    for c0 in range(0, tl, LANES):
        ln = pl.ds(c0, LANES)
        a = largest(s1_ref, ln)
        b = largest(s2_ref, ln)
        cands = [a[i] + b[j] for (i, j) in _CAND]
        best = (neg,) * _NSORT
        for cd in cands:
            best = _insert_sorted(best, cd)
        thr = 0.5 * (best[PEER_TOPK - 1] + best[PEER_TOPK])
        top = a[0] + b[0]
        z = functools.reduce(lambda u, v: u + v, [jnp.where(cd >= thr, jnp.exp(cd - top), 0.0) for cd in cands])
        thr_ref[:, ln] = thr
        top1_ref[:, ln] = a[0]
        top2_ref[:, ln] = b[0]
        zinv_ref[:, ln] = 1.0 / z

    s1 = s1_ref[...]
    tau_out[...] = thr_ref[...][None] - s1
    p1_out[...] = jnp.exp(s1 - top1_ref[...][None])
    for hd in range(PEER_HEADS):
        p2_out[hd] = jnp.exp(s2h_ref[hd] - top2_ref[hd:hd + 1, :]) * (0.5 * zinv_ref[hd:hd + 1, :])


def _topk(s1, s2, s2h, tl):
    T = s2h.shape[2]
    nh_spec = pl.BlockSpec((PEER_NKEYS, PEER_HEADS, tl), lambda i: (0, 0, i))
    hn_spec = pl.BlockSpec((PEER_HEADS, PEER_NKEYS, tl), lambda i: (0, 0, i))
    nh_shape = jax.ShapeDtypeStruct((PEER_NKEYS, PEER_HEADS, T), F32)
    hn_shape = jax.ShapeDtypeStruct((PEER_HEADS, PEER_NKEYS, T), F32)
    return pl.pallas_call(
        functools.partial(_topk_kernel, tl=tl),
        grid=(T // tl,),
        in_specs=[nh_spec, nh_spec, hn_spec],
        out_specs=[nh_spec, nh_spec, hn_spec], out_shape=[nh_shape, nh_shape, hn_shape],
        scratch_shapes=[pltpu.VMEM((PEER_HEADS, tl), F32) for _ in range(4)],
        compiler_params=_params(("parallel",)), name="topk",
    )(s1, s2, s2h)


_ROWS = 32
_MM_ROWS = 256


def _experts_kernel(h2t_ref, u_ref, vt_ref, s2h_ref, tau_ref, p1_ref, p2_ref, x1_ref, gfin_ref,
                    out_ref, acc_ref, at_ref, ct_ref, *, gi, ne):
    j = pl.program_id(1)
    tm = h2t_ref.shape[1]

    @pl.when(j == 0)
    def _():
        acc_ref[...] = jnp.zeros_like(acc_ref)
        at_ref[...] = jnp.zeros_like(at_ref)
        ct_ref[...] = jnp.zeros_like(ct_ref)

    gate_valid = jnp.logical_and(j >= 1, j <= ne)
    jb = jnp.clip(j - 1, 0, ne - 1)

    def stages(slot):
        other = 1 - slot

        def value_rows(r0):
            rows = pl.ds(r0, _MM_ROWS)
            acc_ref[rows, :] += _dot(vt_ref[rows, :], ct_ref[slot])

        def preact_rows(r0):
            rows = pl.ds(r0, _MM_ROWS)
            at_ref[slot, rows, :] = _dot(u_ref[rows, :], h2t_ref[...])

        taus = {}

        def column_taus(c0):
            if c0 not in taus:
                ln = pl.ds(c0, LANES)
                taus[c0] = [jnp.where(gate_valid, tau_ref[jb * gi + g, :, ln], jnp.inf) for g in range(gi)]
            return taus[c0]

        def gate_chunk(c0, r0):
            ln = pl.ds(c0, LANES)
            rw = pl.ds(r0, _ROWS)
            tau = column_taus(c0)
            wgt = [None] * gi
            for hd in range(PEER_HEADS):
                s2 = s2h_ref[hd, rw, ln]
                p2 = p2_ref[hd, rw, ln]
                for g in range(gi):
                    term = jnp.where(s2 >= tau[g][hd:hd + 1], p2, 0.0) * p1_ref[jb * gi + g, pl.ds(hd, 1), ln]
                    wgt[g] = term if wgt[g] is None else wgt[g] + term
            for g in range(gi):
                er = pl.ds(g * PEER_NKEYS + r0, _ROWS)
                a = at_ref[other, er, ln]
                ct_ref[other, er, ln] = (a * (1.0 + lax.erf(a * (2.0 ** -0.5))) * wgt[g]).astype(BF16)

        mxu_tasks = ([functools.partial(value_rows, r0) for r0 in range(0, acc_ref.shape[0], _MM_ROWS)]
                     + [functools.partial(preact_rows, r0) for r0 in range(0, u_ref.shape[0], _MM_ROWS)])
        vpu_tasks = [functools.partial(gate_chunk, c0, r0)
                     for c0 in range(0, tm, LANES) for r0 in range(0, PEER_NKEYS, _ROWS)]
        done = 0
        for n, task in enumerate(vpu_tasks):
            want = ((n + 1) * len(mxu_tasks)) // len(vpu_tasks)
            while done < want:
                mxu_tasks[done]()
                done += 1
            task()

    @pl.when(j % 2 == 0)
    def _():
        stages(0)

    @pl.when(j % 2 == 1)
    def _():
        stages(1)

    @pl.when(j == pl.num_programs(1) - 1)
    def _():
        x = x1_ref[...] + acc_ref[...].T
        out_ref[...] = _rms(x, gfin_ref[...])


def _experts(h2t, u, vt, s2h, tau, p1, p2, x1, gfin, tm, te):
    T, D = x1.shape
    E = u.shape[0]
    gi = te // PEER_NKEYS
    ne = E // te
    tok = lambda n: pl.BlockSpec((tm, n), lambda i, j: (i, 0))
    lanes = lambda a: pl.BlockSpec((a.shape[0], a.shape[1], tm), lambda i, j: (0, 0, i))
    return pl.pallas_call(
        functools.partial(_experts_kernel, gi=gi, ne=ne),
        grid=(T // tm, ne + 2),
        in_specs=[pl.BlockSpec((D, tm), lambda i, j: (0, i)),
                  pl.BlockSpec((te, D), lambda i, j: (jnp.minimum(j, ne - 1), 0)),
                  pl.BlockSpec((D, te), lambda i, j: (0, jnp.clip(j - 2, 0, ne - 1))),
                  lanes(s2h), lanes(tau), lanes(p1), lanes(p2), tok(D), _full(gfin.shape)],
        out_specs=tok(D),
        out_shape=jax.ShapeDtypeStruct((T, D), F32),
        scratch_shapes=[pltpu.VMEM((D, tm), F32), pltpu.VMEM((2, te, tm), F32), pltpu.VMEM((2, te, tm), BF16)],
        compiler_params=_params(("parallel", "arbitrary")), name="experts",
    )(h2t, u, vt, s2h, tau, p1, p2, x1, gfin)


def _rot_half_cols(w):
    half = w.shape[-1] // 2
    return jnp.concatenate([-w[..., half:], w[..., :half]], axis=-1)


def _prep_weights(g_mix, w_in, g_q_lat, w_qb, g_kv_lat, w_kvb, w_a2, b_a2, g_gla, w_branch_a, w_branch_b,
                  w_out, g_ffn, w_peer_q, peer_sub_keys):
    D = w_in.shape[0]
    sizes = (MLA_Q_RANK, MLA_KV_RANK, MLA_ROPE, GLA_HEADS * GLA_DK, GLA_HEADS * GLA_DK, GLA_HEADS * GLA_DV,
             GLA_GATE_RANK, GLA_HEADS * GLA_DV, 2 * D)
    pts = [int(s) for s in np.cumsum(sizes)[:-1]]
    w_ql, w_kvl, w_kr, w_gq, w_gk, w_gv, w_glr, w_go, w_br = jnp.split(w_in, pts, axis=1)
    H = MLA_HEADS
    zpad = lambda *shape: jnp.zeros(shape, F32)
    tail = HEAD_LANES - MLA_NOPE - MLA_ROPE

    def rope_block(wr):
        lead = wr.shape[:-1]
        return jnp.concatenate([zpad(*lead, MLA_NOPE), wr, zpad(*lead, tail)], -1)

    wq3 = w_qb.reshape(MLA_Q_RANK, H, MLA_NOPE + MLA_ROPE)
    q_rope = wq3[..., MLA_NOPE:]
    q0 = jnp.concatenate([wq3[..., :MLA_NOPE], q_rope, zpad(MLA_Q_RANK, H, tail)], -1)
    q1 = rope_block(_rot_half_cols(q_rope))
    wkv3 = w_kvb.reshape(MLA_KV_RANK, H, MLA_NOPE + MLA_V)
    wk = jnp.concatenate([wkv3[..., :MLA_NOPE], zpad(MLA_KV_RANK, H, HEAD_LANES - MLA_NOPE)], -1)
    wv = jnp.concatenate([wkv3[..., MLA_NOPE:], zpad(MLA_KV_RANK, H, HEAD_LANES - MLA_V)], -1)
    wa3 = w_branch_a.reshape(H, MLA_V, D)
    wa = jnp.concatenate([wa3, zpad(H, HEAD_LANES - MLA_V, D)], axis=1).reshape(H * HEAD_LANES, D)

    PH = PEER_HEADS
    wpq = w_peer_q.reshape(D, PH, 2, PEER_HALF).transpose(0, 2, 1, 3).reshape(D, 2 * PH * PEER_HALF)
    eye = jnp.eye(PH, dtype=F32)

    def keys_by_key_head(sk):
        return jnp.einsum('hg,hnd->nhgd', eye, sk).reshape(PEER_NKEYS * PH, PH * PEER_HALF)

    return {
        'g_mix': g_mix.reshape(1, D),
        'w_lat': jnp.concatenate([w_ql, w_kvl], 1).astype(BF16),
        'w_kr': jnp.concatenate([rope_block(w_kr), rope_block(_rot_half_cols(w_kr))], 1).astype(BF16),
        'w_g': jnp.concatenate([w_gq, w_gv, w_go], 1).astype(BF16),
        'w_gt': jnp.concatenate([w_gk, w_glr], 1).T.astype(BF16),
        'w_br': w_br.astype(BF16),
        'g_q_lat': g_q_lat.reshape(1, -1), 'g_kv_lat': g_kv_lat.reshape(1, -1),
        'w_q': q0.reshape(MLA_Q_RANK, H * HEAD_LANES).astype(BF16),
        'w_qr': q1.reshape(MLA_Q_RANK, H * HEAD_LANES).astype(BF16),
        'w_k': wk.reshape(MLA_KV_RANK, H * HEAD_LANES).astype(BF16),
        'w_v': wv.reshape(MLA_KV_RANK, H * HEAD_LANES).astype(BF16),
        'w_a2t': w_a2.T.astype(BF16), 'b_a2col': b_a2.reshape(-1, 1), 'g_gla': g_gla.reshape(1, -1),
        'w_a': wa.astype(BF16), 'w_b': w_branch_b.astype(BF16), 'w_out': w_out.astype(BF16),
        'g_ffn': g_ffn.reshape(1, D), 'w_pq': wpq.astype(BF16),
        'keys1': keys_by_key_head(peer_sub_keys[:, 0]).astype(BF16),
        'keys2': keys_by_key_head(peer_sub_keys[:, 1]).astype(BF16),
        'subk2': peer_sub_keys[:, 1].astype(BF16),
    }


def _rope_freq_lanes():
    half = MLA_ROPE // 2
    inv = ROPE_THETA ** (-np.arange(half, dtype=np.float32) / half)
    lanes = np.zeros((1, HEAD_LANES), np.float32)
    lanes[0, MLA_NOPE:MLA_NOPE + half] = inv
    lanes[0, MLA_NOPE + half:MLA_NOPE + 2 * half] = inv
    return jnp.asarray(lanes)


def _block(x, positions, w, peer_u, peer_v, g_final, *, tm_proj, tq, nsub, tb, tm_merge, tl, tm_exp, te):
    B, S, D = x.shape
    T = B * S
    x2 = x.reshape(T, D)
    posf = positions.reshape(T, 1).astype(F32)
    q, k, v, gq, gkt, gv, glrt, gout, br = _proj(x2, posf, _rope_freq_lanes(), w, tm_proj)
    ya = _attn(q, k, v, B, S, tq, nsub)
    yb = _gla(gq, gkt, gv, glrt, gout, w, B, S, tb)
    x1, h2, s1, s2, s2h = _merge(x2, ya, yb, br, w, tm_merge)
    s1 = s1.reshape(PEER_NKEYS, PEER_HEADS, T)
    s2 = s2.reshape(PEER_NKEYS, PEER_HEADS, T)
    tau, p1, p2 = _topk(s1, s2, s2h, tl)
    out = _experts(h2.T, peer_u.astype(BF16), peer_v.T.astype(BF16), s2h, tau, p1, p2, x1,
                   g_final.reshape(1, D), tm_exp, te)
    return out.reshape(B, S, D)


def kernel(x, positions, g_mix, w_in, g_q_lat, w_qb, g_kv_lat, w_kvb, w_a2, b_a2, g_gla, w_branch_a, w_branch_b, w_out, g_ffn, w_peer_q, peer_sub_keys, peer_u, peer_v, g_final):
    assert g_mix.shape[0] == 1, "single-layer block"
    w = _prep_weights(g_mix[0], w_in[0], g_q_lat[0], w_qb[0], g_kv_lat[0], w_kvb[0], w_a2[0], b_a2[0], g_gla[0],
                      w_branch_a[0], w_branch_b[0], w_out[0], g_ffn[0], w_peer_q[0], peer_sub_keys[0])
    return _block(x, positions, w, peer_u[0], peer_v[0], g_final,
                  tm_proj=256, tq=512, nsub=2, tb=512, tm_merge=256, tl=256, tm_exp=512, te=512)
```

```python
import functools
import math

import numpy as np
import jax
import jax.numpy as jnp
from jax import lax
from jax.experimental import pallas as pl
from jax.experimental.pallas import tpu as pltpu

F32 = jnp.float32
BF16 = jnp.bfloat16

EPS = 1e-6
ROPE_THETA = 10000.0
CHUNK = 64

MLA_HEADS = 8
MLA_Q_RANK = 384
MLA_KV_RANK = 256
MLA_NOPE = 64
MLA_ROPE = 32
MLA_V = 64
HEAD_LANES = 128

GLA_HEADS = 4
GLA_DK = 64
GLA_DV = 128
GLA_GATE_RANK = 16
GLA_TAU = 16.0

PEER_HEADS = 8
PEER_NKEYS = 128
PEER_HALF = 128
PEER_TOPK = 16

LANES = 128
VMEM_LIMIT_BYTES = 56 * 1024 * 1024

_NT = (((1,), (1,)), ((), ()))


def _dot(a, b):
    return jnp.dot(a, b, preferred_element_type=F32)


def _dot_nt(a, b):
    return lax.dot_general(a, b, _NT, preferred_element_type=F32)


def _rms(x, g):
    return x * lax.rsqrt(jnp.mean(x * x, axis=-1, keepdims=True) + EPS) * g


def _params(sem):
    return pltpu.CompilerParams(dimension_semantics=sem, vmem_limit_bytes=VMEM_LIMIT_BYTES)


def _full(shape):
    nd = len(shape)
    return pl.BlockSpec(shape, lambda *_: (0,) * nd)


def _proj_kernel(x_ref, pos_ref, freq_ref, gmix_ref, wlat_ref, wkr_ref, wg_ref, wgt_ref, wbr_ref,
                 gql_ref, wq_ref, wqr_ref, gkv_ref, wk_ref, wv_ref,
                 q_out, k_out, v_out, gq_out, gkt_out, gv_out, glrt_out, gout_out, br_out, *, q_scale):
    h = _rms(x_ref[...], gmix_ref[...]).astype(BF16)
    ang = pos_ref[...] * freq_ref[...]
    cos = jnp.cos(ang)
    sin = jnp.sin(ang)

    lat = _dot(h, wlat_ref[...])
    qn = _rms(lat[:, :MLA_Q_RANK], gql_ref[...]).astype(BF16)
    kvn = _rms(lat[:, MLA_Q_RANK:], gkv_ref[...]).astype(BF16)
    q0 = _dot(qn, wq_ref[...])
    q1 = _dot(qn, wqr_ref[...])
    kr = _dot(h, wkr_ref[...])
    kpe = kr[:, :HEAD_LANES] * cos + kr[:, HEAD_LANES:] * sin
    kk = _dot(kvn, wk_ref[...])
    for hd in range(MLA_HEADS):
        sl = slice(hd * HEAD_LANES, (hd + 1) * HEAD_LANES)
        q_out[:, sl] = ((q0[:, sl] * cos + q1[:, sl] * sin) * q_scale).astype(BF16)
        k_out[:, sl] = (kk[:, sl] + kpe).astype(BF16)
    ones_lane = lax.broadcasted_iota(jnp.int32, (1, MLA_HEADS * HEAD_LANES), 1) % HEAD_LANES == MLA_V
    v_out[...] = (_dot(kvn, wv_ref[...]) + ones_lane.astype(F32)).astype(BF16)

    nqk = GLA_HEADS * GLA_DK
    nv = GLA_HEADS * GLA_DV
    g = _dot(h, wg_ref[...])
    gq_out[...] = (g[:, :nqk] * (GLA_DK ** -0.5)).astype(BF16)
    gv_out[...] = g[:, nqk:nqk + nv].astype(BF16)
    gout_out[...] = g[:, nqk + nv:]
    gt = _dot_nt(wgt_ref[...], h)
    gkt_out[...] = gt[:nqk]
    glrt_out[...] = gt[nqk:].astype(BF16)
    br_out[...] = _dot(h, wbr_ref[...])


def _proj(x2, posf, freq, w, tm):
    T, D = x2.shape
    nqk = GLA_HEADS * GLA_DK
    nv = GLA_HEADS * GLA_DV
    hl = MLA_HEADS * HEAD_LANES
    row = lambda n: pl.BlockSpec((tm, n), lambda i: (i, 0))
    col = lambda n: pl.BlockSpec((n, tm), lambda i: (0, i))
    ins = [x2, posf, freq, w['g_mix'], w['w_lat'], w['w_kr'], w['w_g'], w['w_gt'], w['w_br'],
           w['g_q_lat'], w['w_q'], w['w_qr'], w['g_kv_lat'], w['w_k'], w['w_v']]
    in_specs = [row(D), row(1)] + [_full(a.shape) for a in ins[2:]]
    out_shape = [
        jax.ShapeDtypeStruct((T, hl), BF16), jax.ShapeDtypeStruct((T, hl), BF16),
        jax.ShapeDtypeStruct((T, hl), BF16),
        jax.ShapeDtypeStruct((T, nqk), BF16), jax.ShapeDtypeStruct((nqk, T), F32),
        jax.ShapeDtypeStruct((T, nv), BF16), jax.ShapeDtypeStruct((GLA_GATE_RANK, T), BF16),
        jax.ShapeDtypeStruct((T, nv), F32), jax.ShapeDtypeStruct((T, 2 * D), F32),
    ]
    out_specs = [row(hl), row(hl), row(hl), row(nqk), col(nqk), row(nv), col(GLA_GATE_RANK),
                 row(nv), row(2 * D)]
    q_scale = (MLA_NOPE + MLA_ROPE) ** -0.5 * math.log2(math.e)
    return pl.pallas_call(
        functools.partial(_proj_kernel, q_scale=q_scale),
        grid=(T // tm,), in_specs=in_specs, out_specs=out_specs, out_shape=out_shape,
        compiler_params=_params(("parallel",)), name="proj",
    )(*ins)


def _attn_kernel(q_ref, k_ref, v_ref, o_ref, sa_ref, sb_ref, *, tq, nsub):
    i = pl.program_id(2)
    qs = [q_ref[pl.ds(s * tq, tq), :] for s in range(nsub)]

    def rows(ref, j):
        return ref[pl.ds(pl.multiple_of(j * tq, tq), tq), :]

    def scores(j, dst, subs):
        kj = rows(k_ref, j)
        for s in subs:
            dst[s] = _dot_nt(qs[s], kj)

    def update(carry, s, vj, masked):
        m, acc = carry
        if masked:
            qc = lax.broadcasted_iota(jnp.int32, (tq, tq), 0) // CHUNK
            kc = lax.broadcasted_iota(jnp.int32, (tq, tq), 1) // CHUNK
            s = jnp.where(kc <= qc, s, -1e30)
        m_new = jnp.maximum(m, jnp.max(s, axis=-1, keepdims=True))
        p = jnp.exp2(s - m_new)
        acc = jnp.exp2(m - m_new) * acc + _dot(p.astype(BF16), vj)
        return m_new, acc

    def phase(j, carry, cur, nxt):
        scores(j + 1, nxt, range(nsub))
        vj = rows(v_ref, j)
        return tuple(update(carry[s], cur[s], vj, False) for s in range(nsub))

    def body(jj, carry):
        carry = phase(2 * jj, carry, sa_ref, sb_ref)
        return phase(2 * jj + 1, carry, sb_ref, sa_ref)

    init = tuple((jnp.full((tq, 1), -1e30, F32), jnp.zeros((tq, HEAD_LANES), F32)) for _ in range(nsub))
    scores(0, sa_ref, range(nsub))
    carry = list(lax.fori_loop(0, i, body, init))
    scores(2 * i + 1, sb_ref, [1])
    v0 = rows(v_ref, 2 * i)
    carry[0] = update(carry[0], sa_ref[0], v0, True)
    carry[1] = update(carry[1], sa_ref[1], v0, False)
    carry[1] = update(carry[1], sb_ref[1], rows(v_ref, 2 * i + 1), True)
    lane = lax.broadcasted_iota(jnp.int32, (tq, HEAD_LANES), 1)
    for s in range(nsub):
        acc = carry[s][1]
        o = acc / acc[:, MLA_V:MLA_V + 1]
        o_ref[pl.ds(s * tq, tq), :] = jnp.where(lane < MLA_V, o, 0.0).astype(BF16)


def _attn(q, k, v, B, S, tq, nsub):
    assert nsub == 2, "the score double-buffering is written for two query sub-tiles per step"
    T = q.shape[0]
    nq = S // (tq * nsub)
    qspec = pl.BlockSpec((tq * nsub, HEAD_LANES), lambda b, h, i: (b * nq + i, h))
    kvspec = pl.BlockSpec((S, HEAD_LANES), lambda b, h, i: (b, h))
    return pl.pallas_call(
        functools.partial(_attn_kernel, tq=tq, nsub=nsub),
        grid=(B, MLA_HEADS, nq), in_specs=[qspec, kvspec, kvspec], out_specs=qspec,
        out_shape=jax.ShapeDtypeStruct((T, MLA_HEADS * HEAD_LANES), BF16),
        scratch_shapes=[pltpu.VMEM((nsub, tq, tq), F32), pltpu.VMEM((nsub, tq, tq), F32)],
        compiler_params=_params(("parallel", "parallel", "arbitrary")), name="attn",
    )(q, k, v)


def _split3(x):
    hi = x.astype(BF16)
    r = x - hi.astype(F32)
    mid = r.astype(BF16)
    lo = (r - mid.astype(F32)).astype(BF16)
    return hi, mid, lo


def _gla_kernel(gq_ref, gkt_ref, gv_ref, glrt_ref, gout_ref, wat_ref, bcol_ref, ggn_ref, sufm_ref, chm_ref,
                y_ref, st_ref, *, tb):
    @pl.when(pl.program_id(1) == 0)
    def _():
        st_ref[...] = jnp.zeros_like(st_ref)

    z = _dot(wat_ref[...], glrt_ref[...]) + bcol_ref[...]
    la = jax.nn.log_sigmoid(z) * (1.0 / GLA_TAU)
    parts = _split3(la)
    sufm = sufm_ref[...]
    chm = chm_ref[...]
    suf = _dot(parts[0], sufm) + _dot(parts[1], sufm) + _dot(parts[2], sufm)
    tot = _dot(parts[0], chm) + _dot(parts[1], chm) + _dot(parts[2], chm)
    kdec = (gkt_ref[...] * jnp.exp(suf)).astype(BF16)
    dec = jnp.exp(tot)
    gq = gq_ref[...]
    gv = gv_ref[...]
    for c in range(tb // CHUNK):
        fr = slice(c * CHUNK, (c + 1) * CHUNK)
        for hd in range(GLA_HEADS):
            kr = slice(hd * GLA_DK, (hd + 1) * GLA_DK)
            vr = slice(hd * GLA_DV, (hd + 1) * GLA_DV)
            ds = _dot(kdec[kr, fr], gv[fr, vr])
            st = dec[kr, c:c + 1] * st_ref[hd] + ds
            st_ref[hd] = st
            o = _dot(gq[fr, kr], st.astype(BF16))
            o = o * lax.rsqrt(jnp.mean(o * o, axis=-1, keepdims=True) + EPS) * ggn_ref[:, vr]
            y_ref[fr, vr] = (o * jax.nn.silu(gout_ref[fr, vr])).astype(BF16)


def _gla(gq, gkt, gv, glrt, gout, w, B, S, tb):
    T = gq.shape[0]
    nb = S // tb
    nqk = GLA_HEADS * GLA_DK
    nv = GLA_HEADS * GLA_DV
    row = lambda n: pl.BlockSpec((tb, n), lambda b, j: (b * nb + j, 0))
    col = lambda n: pl.BlockSpec((n, tb), lambda b, j: (0, b * nb + j))
    fr = np.arange(tb)
    same = (fr[:, None] // CHUNK) == (fr[None, :] // CHUNK)
    sufm = jnp.asarray(same & (fr[:, None] > fr[None, :]), BF16)
    chm = jnp.asarray((fr[:, None] // CHUNK) == np.arange(LANES)[None, :], BF16)
    consts = [w['w_a2t'], w['b_a2col'], w['g_gla'], sufm, chm]
    return pl.pallas_call(
        functools.partial(_gla_kernel, tb=tb),
        grid=(B, nb),
        in_specs=[row(nqk), col(nqk), row(nv), col(GLA_GATE_RANK), row(nv)] + [_full(a.shape) for a in consts],
        out_specs=row(nv),
        out_shape=jax.ShapeDtypeStruct((T, nv), BF16),
        scratch_shapes=[pltpu.VMEM((GLA_HEADS, GLA_DK, GLA_DV), F32)],
        compiler_params=_params(("parallel", "arbitrary")), name="gla",
    )(gq, gkt, gv, glrt, gout, *consts)


def _merge_kernel(x_ref, ya_ref, yb_ref, br_ref, wa_ref, wb_ref, wo_ref, gffn_ref, wpq_ref, k1_ref, k2_ref, subk2_ref,
                  x1_out, h2_out, s1_out, s2_out, s2h_out):
    D = x_ref.shape[1]
    gates = jax.nn.sigmoid(br_ref[...])
    merged = gates[:, :D] * _dot(ya_ref[...], wa_ref[...]) + gates[:, D:] * _dot(yb_ref[...], wb_ref[...])
    x1 = x_ref[...] + _dot(merged.astype(BF16), wo_ref[...])
    x1_out[...] = x1
    h2 = _rms(x1, gffn_ref[...]).astype(BF16)
    h2_out[...] = h2
    pq = _dot(h2, wpq_ref[...]).astype(BF16)
    nh = PEER_HEADS * PEER_HALF
    s1_out[...] = _dot_nt(k1_ref[...], pq[:, :nh])
    s2_out[...] = _dot_nt(k2_ref[...], pq[:, nh:])
    for hd in range(PEER_HEADS):
        s2h_out[hd] = _dot_nt(subk2_ref[hd], pq[:, nh + hd * PEER_HALF:nh + (hd + 1) * PEER_HALF])


def _merge(x2, ya, yb, br, w, tm):
    T, D = x2.shape
    row = lambda n: pl.BlockSpec((tm, n), lambda i: (i, 0))
    col = lambda n: pl.BlockSpec((n, tm), lambda i: (0, i))
    nkh = PEER_NKEYS * PEER_HEADS
    consts = [w['w_a'], w['w_b'], w['w_out'], w['g_ffn'], w['w_pq'], w['keys1'], w['keys2'], w['subk2']]
    return pl.pallas_call(
        _merge_kernel,
        grid=(T // tm,),
        in_specs=[row(D), row(ya.shape[1]), row(yb.shape[1]), row(2 * D)] + [_full(a.shape) for a in consts],
        out_specs=[row(D), row(D), col(nkh), col(nkh),
                   pl.BlockSpec((PEER_HEADS, PEER_NKEYS, tm), lambda i: (0, 0, i))],
        out_shape=[jax.ShapeDtypeStruct((T, D), F32), jax.ShapeDtypeStruct((T, D), BF16),
                   jax.ShapeDtypeStruct((nkh, T), F32), jax.ShapeDtypeStruct((nkh, T), F32),
                   jax.ShapeDtypeStruct((PEER_HEADS, PEER_NKEYS, T), F32)],
        compiler_params=_params(("parallel",)), name="merge",
    )(x2, ya, yb, br, *consts)


_NSORT = PEER_TOPK + 1
_CAND = [(i, j) for i in range(_NSORT) for j in range(_NSORT) if (i + 1) * (j + 1) <= _NSORT]


def _insert_sorted(ms, x):
    out = []
    for m in ms:
        out.append(jnp.maximum(m, x))
        x = jnp.minimum(m, x)
    return tuple(out)


def _bitonic_merge(xs):
    n = len(xs)
    if n == 1:
        return xs
    half = n // 2
    hi = [jnp.maximum(xs[i], xs[i + half]) for i in range(half)]
    lo = [jnp.minimum(xs[i], xs[i + half]) for i in range(half)]
    return _bitonic_merge(hi) + _bitonic_merge(lo)


def _bitonic_sort(xs):
    if len(xs) == 1:
        return xs
    half = len(xs) // 2
    return _bitonic_merge(_bitonic_sort(xs[:half]) + _bitonic_sort(xs[half:])[::-1])


def _merge_top(t1, e1, t2, e2):
    k = len(t1)
    hi = [jnp.maximum(t1[i], t2[k - 1 - i]) for i in range(k)]
    lo = [jnp.minimum(t1[i], t2[k - 1 - i]) for i in range(k)]
    return _bitonic_merge(hi), functools.reduce(jnp.maximum, lo + [e1, e2])


def _count_sorted(v, x, strict):
    above = (lambda a: a > x) if strict else (lambda a: a >= x)
    c = [above(v[4 * i + 3]) for i in range(4)]
    pick = lambda j: jnp.where(c[2], v[12 + j], jnp.where(c[1], v[8 + j], jnp.where(c[0], v[4 + j], v[j])))
    fine = jnp.where(above(pick(2)), 3.0, jnp.where(above(pick(1)), 2.0, jnp.where(above(pick(0)), 1.0, 0.0)))
    base = jnp.where(c[2], 12.0, jnp.where(c[1], 8.0, jnp.where(c[0], 4.0, 0.0)))
    return jnp.where(c[3], 16.0, base + fine)


_KEY_ROWS = 16
_RANK_ROWS = 64


def _topk_kernel(s1_ref, s2_ref, s2h_ref, cnt_out, p1_out, rank_out, p2_out,
                 thr_ref, top1_ref, zinv_ref, b_ref, mid_ref, *, tl):
    neg = jnp.full((PEER_HEADS, LANES), -jnp.inf, F32)

    def largest(s_ref, ln):
        def summary(lo, hi):
            if hi - lo == PEER_TOPK:
                return _bitonic_sort([s_ref[n, :, ln] for n in range(lo, hi)]), neg
            mid = (lo + hi) // 2
            return _merge_top(*summary(lo, mid), *summary(mid, hi))

        top, nxt = summary(0, PEER_NKEYS)
        return tuple(top) + (nxt,)

    for c0 in range(0, tl, LANES):
        ln = pl.ds(c0, LANES)
        a = largest(s1_ref, ln)
        b = largest(s2_ref, ln)
        cands = [a[i] + b[j] for (i, j) in _CAND]
        best = (neg,) * _NSORT
        for cd in cands:
            best = _insert_sorted(best, cd)
        thr = 0.5 * (best[PEER_TOPK - 1] + best[PEER_TOPK])
        top = a[0] + b[0]
        z = functools.reduce(lambda u, v: u + v, [jnp.where(cd >= thr, jnp.exp(cd - top), 0.0) for cd in cands])
        thr_ref[:, ln] = thr
        top1_ref[:, ln] = a[0]
        zinv_ref[:, ln] = 0.5 / z
        for kk in range(PEER_TOPK):
            b_ref[kk, :, ln] = b[kk]
            mid_ref[kk, :, ln] = 0.5 * (b[kk] + b[kk + 1])

    def count_rows(c, _):
        rows = pl.ds(pl.multiple_of(c * _KEY_ROWS, _KEY_ROWS), _KEY_ROWS)
        s1 = s1_ref[rows]
        tau = thr_ref[...][None] - s1
        cnt_out[rows] = _count_sorted([b_ref[kk][None] for kk in range(PEER_TOPK)], tau, strict=False)
        p1_out[rows] = jnp.exp(s1 - top1_ref[...][None])
        return 0

    lax.fori_loop(0, PEER_NKEYS // _KEY_ROWS, count_rows, 0)

    for hd in range(PEER_HEADS):
        def rank_rows(c, _, hd=hd):
            rows = pl.ds(pl.multiple_of(c * _RANK_ROWS, _RANK_ROWS), _RANK_ROWS)
            s2 = s2h_ref[hd, rows, :]
            mids = [mid_ref[kk, hd:hd + 1, :] for kk in range(PEER_TOPK)]
            rank_out[hd, rows, :] = _count_sorted(mids, s2, strict=True)
            p2_out[hd, rows, :] = jnp.exp(s2 - b_ref[0, hd:hd + 1, :]) * zinv_ref[hd:hd + 1, :]
            return 0

        lax.fori_loop(0, PEER_NKEYS // _RANK_ROWS, rank_rows, 0)


def _topk(s1, s2, s2h, tl):
    T = s2h.shape[2]
    nh_spec = pl.BlockSpec((PEER_NKEYS, PEER_HEADS, tl), lambda i: (0, 0, i))
    hn_spec = pl.BlockSpec((PEER_HEADS, PEER_NKEYS, tl), lambda i: (0, 0, i))
    nh_shape = jax.ShapeDtypeStruct((PEER_NKEYS, PEER_HEADS, T), F32)
    hn_shape = jax.ShapeDtypeStruct((PEER_HEADS, PEER_NKEYS, T), F32)
    return pl.pallas_call(
        functools.partial(_topk_kernel, tl=tl),
        grid=(T // tl,),
        in_specs=[nh_spec, nh_spec, hn_spec],
        out_specs=[nh_spec, nh_spec, hn_spec, hn_spec], out_shape=[nh_shape, nh_shape, hn_shape, hn_shape],
        scratch_shapes=[pltpu.VMEM((PEER_HEADS, tl), F32) for _ in range(3)]
                       + [pltpu.VMEM((PEER_TOPK, PEER_HEADS, tl), F32) for _ in range(2)],
        compiler_params=_params(("parallel",)), name="topk",
    )(s1, s2, s2h)


_ROWS = 32
_PACK = 16
_MM_ROWS = 128


def _experts_kernel(h2t_ref, u_ref, vt_ref, rank_ref, cnt_ref, p1_ref, p2_ref, x1_ref, gfin_ref,
                    out_ref, acc_ref, at_ref, ct_ref, cb_ref, pb_ref, *, gi, ne):
    j = pl.program_id(1)
    tm = h2t_ref.shape[1]

    @pl.when(j == 0)
    def _():
        acc_ref[...] = jnp.zeros_like(acc_ref)
        at_ref[...] = jnp.zeros_like(at_ref)
        ct_ref[...] = jnp.zeros_like(ct_ref)

    gate_valid = jnp.logical_and(j >= 1, j <= ne)
    jb = jnp.clip(j - 1, 0, ne - 1)

    def stages(slot):
        other = 1 - slot

        def value_rows(r0):
            rows = pl.ds(r0, _MM_ROWS)
            acc_ref[rows, :] += _dot(vt_ref[rows, :], ct_ref[slot])

        def preact_rows(r0):
            rows = pl.ds(r0, _MM_ROWS)
            at_ref[slot, rows, :] = _dot(u_ref[rows, :], h2t_ref[...])

        def column_rows(c0):
            ci = c0 // LANES
            ln = pl.ds(c0, LANES)
            for g in range(gi):
                cnt = jnp.where(gate_valid, cnt_ref[jb * gi + g, :, ln], 0.0)
                p1 = p1_ref[jb * gi + g, :, ln]
                for hd in range(PEER_HEADS):
                    cb_ref[ci, g * PEER_HEADS + hd] = jnp.broadcast_to(cnt[hd:hd + 1], (_PACK, LANES)).astype(BF16)
                    pb_ref[ci, g * PEER_HEADS + hd] = jnp.broadcast_to(p1[hd:hd + 1], (_PACK, LANES)).astype(BF16)

        def gate_chunk(c0, r0):
            if r0 == 0:
                column_rows(c0)
            ci = c0 // LANES
            ln = pl.ds(c0, LANES)
            subs = range(r0, r0 + _ROWS, _PACK)
            wgt = [[None] * len(subs) for _ in range(gi)]
            for hd in range(PEER_HEADS):
                rank = [rank_ref[hd, pl.ds(r, _PACK), ln].astype(BF16) for r in subs]
                p2 = [p2_ref[hd, pl.ds(r, _PACK), ln].astype(BF16) for r in subs]
                for g in range(gi):
                    cnt = cb_ref[ci, g * PEER_HEADS + hd]
                    p1 = pb_ref[ci, g * PEER_HEADS + hd]
                    for k in range(len(subs)):
                        term = jnp.where(rank[k] < cnt, p2[k], jnp.zeros_like(p2[k])) * p1
                        wgt[g][k] = term if wgt[g][k] is None else wgt[g][k] + term
            for g in range(gi):
                for k, r in enumerate(subs):
                    er = pl.ds(g * PEER_NKEYS + r, _PACK)
                    a = at_ref[other, er, ln]
                    act = a * (1.0 + lax.erf(a * (2.0 ** -0.5)))
                    ct_ref[other, er, ln] = act.astype(BF16) * wgt[g][k]

        mxu_tasks = ([functools.partial(value_rows, r0) for r0 in range(0, acc_ref.shape[0], _MM_ROWS)]
                     + [functools.partial(preact_rows, r0) for r0 in range(0, u_ref.shape[0], _MM_ROWS)])
        vpu_tasks = [functools.partial(gate_chunk, c0, r0)
                     for c0 in range(0, tm, LANES) for r0 in range(0, PEER_NKEYS, _ROWS)]
        done = 0
        for n, task in enumerate(vpu_tasks):
            want = ((n + 1) * len(mxu_tasks)) // len(vpu_tasks)
            while done < want:
                mxu_tasks[done]()
                done += 1
            task()

    @pl.when(j % 2 == 0)
    def _():
        stages(0)

    @pl.when(j % 2 == 1)
    def _():
        stages(1)

    @pl.when(j == pl.num_programs(1) - 1)
    def _():
        x = x1_ref[...] + acc_ref[...].T
        out_ref[...] = _rms(x, gfin_ref[...])


def _experts(h2t, u, vt, rank, cnt, p1, p2, x1, gfin, tm, te):
    T, D = x1.shape
    E = u.shape[0]
    gi = te // PEER_NKEYS
    ne = E // te
    tok = lambda n: pl.BlockSpec((tm, n), lambda i, j: (i, 0))
    lanes = lambda a: pl.BlockSpec((a.shape[0], a.shape[1], tm), lambda i, j: (0, 0, i))
    row_tables = pltpu.VMEM((tm // LANES, gi * PEER_HEADS, _PACK, LANES), BF16)
    return pl.pallas_call(
        functools.partial(_experts_kernel, gi=gi, ne=ne),
        grid=(T // tm, ne + 2),
        in_specs=[pl.BlockSpec((D, tm), lambda i, j: (0, i)),
                  pl.BlockSpec((te, D), lambda i, j: (jnp.minimum(j, ne - 1), 0)),
                  pl.BlockSpec((D, te), lambda i, j: (0, jnp.clip(j - 2, 0, ne - 1))),
                  lanes(rank), lanes(cnt), lanes(p1), lanes(p2), tok(D), _full(gfin.shape)],
        out_specs=tok(D),
        out_shape=jax.ShapeDtypeStruct((T, D), F32),
        scratch_shapes=[pltpu.VMEM((D, tm), F32), pltpu.VMEM((2, te, tm), F32), pltpu.VMEM((2, te, tm), BF16),
                        row_tables, row_tables],
        compiler_params=_params(("parallel", "arbitrary")), name="experts",
    )(h2t, u, vt, rank, cnt, p1, p2, x1, gfin)


def _rot_half_cols(w):
    half = w.shape[-1] // 2
    return jnp.concatenate([-w[..., half:], w[..., :half]], axis=-1)


def _prep_weights(g_mix, w_in, g_q_lat, w_qb, g_kv_lat, w_kvb, w_a2, b_a2, g_gla, w_branch_a, w_branch_b,
                  w_out, g_ffn, w_peer_q, peer_sub_keys):
    D = w_in.shape[0]
    sizes = (MLA_Q_RANK, MLA_KV_RANK, MLA_ROPE, GLA_HEADS * GLA_DK, GLA_HEADS * GLA_DK, GLA_HEADS * GLA_DV,
             GLA_GATE_RANK, GLA_HEADS * GLA_DV, 2 * D)
    pts = [int(s) for s in np.cumsum(sizes)[:-1]]
    w_ql, w_kvl, w_kr, w_gq, w_gk, w_gv, w_glr, w_go, w_br = jnp.split(w_in, pts, axis=1)
    H = MLA_HEADS
    zpad = lambda *shape: jnp.zeros(shape, F32)
    tail = HEAD_LANES - MLA_NOPE - MLA_ROPE

    def rope_block(wr):
        lead = wr.shape[:-1]
        return jnp.concatenate([zpad(*lead, MLA_NOPE), wr, zpad(*lead, tail)], -1)

    wq3 = w_qb.reshape(MLA_Q_RANK, H, MLA_NOPE + MLA_ROPE)
    q_rope = wq3[..., MLA_NOPE:]
    q0 = jnp.concatenate([wq3[..., :MLA_NOPE], q_rope, zpad(MLA_Q_RANK, H, tail)], -1)
    q1 = rope_block(_rot_half_cols(q_rope))
    wkv3 = w_kvb.reshape(MLA_KV_RANK, H, MLA_NOPE + MLA_V)
    wk = jnp.concatenate([wkv3[..., :MLA_NOPE], zpad(MLA_KV_RANK, H, HEAD_LANES - MLA_NOPE)], -1)
    wv = jnp.concatenate([wkv3[..., MLA_NOPE:], zpad(MLA_KV_RANK, H, HEAD_LANES - MLA_V)], -1)
    wa3 = w_branch_a.reshape(H, MLA_V, D)
    wa = jnp.concatenate([wa3, zpad(H, HEAD_LANES - MLA_V, D)], axis=1).reshape(H * HEAD_LANES, D)

    PH = PEER_HEADS
    wpq = w_peer_q.reshape(D, PH, 2, PEER_HALF).transpose(0, 2, 1, 3).reshape(D, 2 * PH * PEER_HALF)
    eye = jnp.eye(PH, dtype=F32)

    def keys_by_key_head(sk):
        return jnp.einsum('hg,hnd->nhgd', eye, sk).reshape(PEER_NKEYS * PH, PH * PEER_HALF)

    return {
        'g_mix': g_mix.reshape(1, D),
        'w_lat': jnp.concatenate([w_ql, w_kvl], 1).astype(BF16),
        'w_kr': jnp.concatenate([rope_block(w_kr), rope_block(_rot_half_cols(w_kr))], 1).astype(BF16),
        'w_g': jnp.concatenate([w_gq, w_gv, w_go], 1).astype(BF16),
        'w_gt': jnp.concatenate([w_gk, w_glr], 1).T.astype(BF16),
        'w_br': w_br.astype(BF16),
        'g_q_lat': g_q_lat.reshape(1, -1), 'g_kv_lat': g_kv_lat.reshape(1, -1),
        'w_q': q0.reshape(MLA_Q_RANK, H * HEAD_LANES).astype(BF16),
        'w_qr': q1.reshape(MLA_Q_RANK, H * HEAD_LANES).astype(BF16),
        'w_k': wk.reshape(MLA_KV_RANK, H * HEAD_LANES).astype(BF16),
        'w_v': wv.reshape(MLA_KV_RANK, H * HEAD_LANES).astype(BF16),
        'w_a2t': w_a2.T.astype(BF16), 'b_a2col': b_a2.reshape(-1, 1), 'g_gla': g_gla.reshape(1, -1),
        'w_a': wa.astype(BF16), 'w_b': w_branch_b.astype(BF16), 'w_out': w_out.astype(BF16),
        'g_ffn': g_ffn.reshape(1, D), 'w_pq': wpq.astype(BF16),
        'keys1': keys_by_key_head(peer_sub_keys[:, 0]).astype(BF16),
        'keys2': keys_by_key_head(peer_sub_keys[:, 1]).astype(BF16),
        'subk2': peer_sub_keys[:, 1].astype(BF16),
    }


def _rope_freq_lanes():
    half = MLA_ROPE // 2
    inv = ROPE_THETA ** (-np.arange(half, dtype=np.float32) / half)
    lanes = np.zeros((1, HEAD_LANES), np.float32)
    lanes[0, MLA_NOPE:MLA_NOPE + half] = inv
    lanes[0, MLA_NOPE + half:MLA_NOPE + 2 * half] = inv
    return jnp.asarray(lanes)


def _block(x, positions, w, peer_u, peer_v, g_final, *, tm_proj, tq, nsub, tb, tm_merge, tl, tm_exp, te):
    B, S, D = x.shape
    T = B * S
    x2 = x.reshape(T, D)
    posf = positions.reshape(T, 1).astype(F32)
    q, k, v, gq, gkt, gv, glrt, gout, br = _proj(x2, posf, _rope_freq_lanes(), w, tm_proj)
    ya = _attn(q, k, v, B, S, tq, nsub)
    yb = _gla(gq, gkt, gv, glrt, gout, w, B, S, tb)
    x1, h2, s1, s2, s2h = _merge(x2, ya, yb, br, w, tm_merge)
    s1 = s1.reshape(PEER_NKEYS, PEER_HEADS, T)
    s2 = s2.reshape(PEER_NKEYS, PEER_HEADS, T)
    cnt, p1, rank, p2 = _topk(s1, s2, s2h, tl)
    out = _experts(h2.T, peer_u.astype(BF16), peer_v.T.astype(BF16), rank, cnt, p1, p2, x1,
                   g_final.reshape(1, D), tm_exp, te)
    return out.reshape(B, S, D)


def kernel(x, positions, g_mix, w_in, g_q_lat, w_qb, g_kv_lat, w_kvb, w_a2, b_a2, g_gla, w_branch_a, w_branch_b, w_out, g_ffn, w_peer_q, peer_sub_keys, peer_u, peer_v, g_final):
    assert g_mix.shape[0] == 1, "single-layer block"
    w = _prep_weights(g_mix[0], w_in[0], g_q_lat[0], w_qb[0], g_kv_lat[0], w_kvb[0], w_a2[0], b_a2[0], g_gla[0],
                      w_branch_a[0], w_branch_b[0], w_out[0], g_ffn[0], w_peer_q[0], peer_sub_keys[0])
    return _block(x, positions, w, peer_u[0], peer_v[0], g_final,
                  tm_proj=256, tq=512, nsub=2, tb=512, tm_merge=256, tl=256, tm_exp=512, te=1024)
```

```python
import functools
import math

import numpy as np
import jax
import jax.numpy as jnp
from jax import lax
from jax.experimental import pallas as pl
from jax.experimental.pallas import tpu as pltpu

F32 = jnp.float32
BF16 = jnp.bfloat16

EPS = 1e-6
ROPE_THETA = 10000.0
CHUNK = 64

MLA_HEADS = 8
MLA_Q_RANK = 384
MLA_KV_RANK = 256
MLA_NOPE = 64
MLA_ROPE = 32
MLA_V = 64
HEAD_LANES = 128

GLA_HEADS = 4
GLA_DK = 64
GLA_DV = 128
GLA_GATE_RANK = 16
GLA_TAU = 16.0

PEER_HEADS = 8
PEER_NKEYS = 128
PEER_HALF = 128
PEER_TOPK = 16

LANES = 128
VMEM_LIMIT_BYTES = 56 * 1024 * 1024

_NT = (((1,), (1,)), ((), ()))


def _dot(a, b):
    return jnp.dot(a, b, preferred_element_type=F32)


def _dot_nt(a, b):
    return lax.dot_general(a, b, _NT, preferred_element_type=F32)


def _rms(x, g):
    return x * lax.rsqrt(jnp.mean(x * x, axis=-1, keepdims=True) + EPS) * g


def _params(sem):
    return pltpu.CompilerParams(dimension_semantics=sem, vmem_limit_bytes=VMEM_LIMIT_BYTES)


def _full(shape):
    nd = len(shape)
    return pl.BlockSpec(shape, lambda *_: (0,) * nd, pipeline_mode=pl.Buffered(1))


def _proj_kernel(x_ref, pos_ref, freq_ref, gmix_ref, wlat_ref, wkr_ref, wg_ref, wgt_ref, wbr_ref,
                 gql_ref, wq_ref, wqr_ref, gkv_ref, wk_ref, wv_ref,
                 q_out, k_out, v_out, gq_out, gkt_out, gv_out, glrt_out, gout_out, br_out, *, q_scale):
    h = _rms(x_ref[...], gmix_ref[...]).astype(BF16)
    ang = pos_ref[...] * freq_ref[...]
    cos = jnp.cos(ang)
    sin = jnp.sin(ang)

    lat = _dot(h, wlat_ref[...])
    qn = _rms(lat[:, :MLA_Q_RANK], gql_ref[...]).astype(BF16)
    kvn = _rms(lat[:, MLA_Q_RANK:], gkv_ref[...]).astype(BF16)
    q0 = _dot(qn, wq_ref[...])
    q1 = _dot(qn, wqr_ref[...])
    kr = _dot(h, wkr_ref[...])
    kpe = kr[:, :HEAD_LANES] * cos + kr[:, HEAD_LANES:] * sin
    kk = _dot(kvn, wk_ref[...])
    for hd in range(MLA_HEADS):
        sl = slice(hd * HEAD_LANES, (hd + 1) * HEAD_LANES)
        q_out[:, sl] = ((q0[:, sl] * cos + q1[:, sl] * sin) * q_scale).astype(BF16)
        k_out[:, sl] = (kk[:, sl] + kpe).astype(BF16)
    ones_lane = lax.broadcasted_iota(jnp.int32, (1, MLA_HEADS * HEAD_LANES), 1) % HEAD_LANES == MLA_V
    v_out[...] = (_dot(kvn, wv_ref[...]) + ones_lane.astype(F32)).astype(BF16)

    nqk = GLA_HEADS * GLA_DK
    nv = GLA_HEADS * GLA_DV
    g = _dot(h, wg_ref[...])
    gq_out[...] = (g[:, :nqk] * (GLA_DK ** -0.5)).astype(BF16)
    gv_out[...] = g[:, nqk:nqk + nv].astype(BF16)
    gout_out[...] = g[:, nqk + nv:]
    gt = _dot_nt(wgt_ref[...], h)
    gkt_out[...] = gt[:nqk]
    glrt_out[...] = gt[nqk:].astype(BF16)
    br_out[...] = _dot(h, wbr_ref[...])


def _proj(x2, posf, freq, w, tm):
    T, D = x2.shape
    nqk = GLA_HEADS * GLA_DK
    nv = GLA_HEADS * GLA_DV
    hl = MLA_HEADS * HEAD_LANES
    row = lambda n: pl.BlockSpec((tm, n), lambda i: (i, 0))
    col = lambda n: pl.BlockSpec((n, tm), lambda i: (0, i))
    ins = [x2, posf, freq, w['g_mix'], w['w_lat'], w['w_kr'], w['w_g'], w['w_gt'], w['w_br'],
           w['g_q_lat'], w['w_q'], w['w_qr'], w['g_kv_lat'], w['w_k'], w['w_v']]
    in_specs = [row(D), row(1)] + [_full(a.shape) for a in ins[2:]]
    out_shape = [
        jax.ShapeDtypeStruct((T, hl), BF16), jax.ShapeDtypeStruct((T, hl), BF16),
        jax.ShapeDtypeStruct((T, hl), BF16),
        jax.ShapeDtypeStruct((T, nqk), BF16), jax.ShapeDtypeStruct((nqk, T), F32),
        jax.ShapeDtypeStruct((T, nv), BF16), jax.ShapeDtypeStruct((GLA_GATE_RANK, T), BF16),
        jax.ShapeDtypeStruct((T, nv), F32), jax.ShapeDtypeStruct((T, 2 * D), F32),
    ]
    out_specs = [row(hl), row(hl), row(hl), row(nqk), col(nqk), row(nv), col(GLA_GATE_RANK),
                 row(nv), row(2 * D)]
    q_scale = (MLA_NOPE + MLA_ROPE) ** -0.5 * math.log2(math.e)
    return pl.pallas_call(
        functools.partial(_proj_kernel, q_scale=q_scale),
        grid=(T // tm,), in_specs=in_specs, out_specs=out_specs, out_shape=out_shape,
        compiler_params=_params(("parallel",)), name="proj",
    )(*ins)


def _attn_kernel(q_ref, k_ref, v_ref, o_ref, sa_ref, sb_ref, *, tq, nsub):
    i = pl.program_id(2)
    qs = [q_ref[pl.ds(s * tq, tq), :] for s in range(nsub)]

    def rows(ref, j):
        return ref[pl.ds(pl.multiple_of(j * tq, tq), tq), :]

    def scores(j, dst, subs):
        kj = rows(k_ref, j)
        for s in subs:
            dst[s] = _dot_nt(qs[s], kj)

    def update(carry, s, vj, masked):
        m, acc = carry
        if masked:
            qc = lax.broadcasted_iota(jnp.int32, (tq, tq), 0) // CHUNK
            kc = lax.broadcasted_iota(jnp.int32, (tq, tq), 1) // CHUNK
            s = jnp.where(kc <= qc, s, -1e30)
        m_new = jnp.maximum(m, jnp.max(s, axis=-1, keepdims=True))
        p = jnp.exp2(s - m_new)
        acc = jnp.exp2(m - m_new) * acc + _dot(p.astype(BF16), vj)
        return m_new, acc

    def phase(j, carry, cur, nxt):
        scores(j + 1, nxt, range(nsub))
        vj = rows(v_ref, j)
        return tuple(update(carry[s], cur[s], vj, False) for s in range(nsub))

    def body(jj, carry):
        carry = phase(2 * jj, carry, sa_ref, sb_ref)
        return phase(2 * jj + 1, carry, sb_ref, sa_ref)

    init = tuple((jnp.full((tq, 1), -1e30, F32), jnp.zeros((tq, HEAD_LANES), F32)) for _ in range(nsub))
    scores(0, sa_ref, range(nsub))
    carry = list(lax.fori_loop(0, i, body, init))
    scores(2 * i + 1, sb_ref, [1])
    v0 = rows(v_ref, 2 * i)
    carry[0] = update(carry[0], sa_ref[0], v0, True)
    carry[1] = update(carry[1], sa_ref[1], v0, False)
    carry[1] = update(carry[1], sb_ref[1], rows(v_ref, 2 * i + 1), True)
    lane = lax.broadcasted_iota(jnp.int32, (tq, HEAD_LANES), 1)
    for s in range(nsub):
        acc = carry[s][1]
        o = acc / acc[:, MLA_V:MLA_V + 1]
        o_ref[pl.ds(s * tq, tq), :] = jnp.where(lane < MLA_V, o, 0.0).astype(BF16)


def _attn(q, k, v, B, S, tq, nsub):
    assert nsub == 2, "the score double-buffering is written for two query sub-tiles per step"
    T = q.shape[0]
    nq = S // (tq * nsub)
    qspec = pl.BlockSpec((tq * nsub, HEAD_LANES), lambda b, h, i: (b * nq + i, h))
    kvspec = pl.BlockSpec((S, HEAD_LANES), lambda b, h, i: (b, h))
    return pl.pallas_call(
        functools.partial(_attn_kernel, tq=tq, nsub=nsub),
        grid=(B, MLA_HEADS, nq), in_specs=[qspec, kvspec, kvspec], out_specs=qspec,
        out_shape=jax.ShapeDtypeStruct((T, MLA_HEADS * HEAD_LANES), BF16),
        scratch_shapes=[pltpu.VMEM((nsub, tq, tq), F32), pltpu.VMEM((nsub, tq, tq), F32)],
        compiler_params=_params(("parallel", "parallel", "arbitrary")), name="attn",
    )(q, k, v)


def _split3(x):
    hi = x.astype(BF16)
    r = x - hi.astype(F32)
    mid = r.astype(BF16)
    lo = (r - mid.astype(F32)).astype(BF16)
    return hi, mid, lo


def _gla_kernel(gq_ref, gkt_ref, gv_ref, glrt_ref, gout_ref, wat_ref, bcol_ref, ggn_ref, sufm_ref, chm_ref,
                y_ref, st_ref, *, tb):
    @pl.when(pl.program_id(1) == 0)
    def _():
        st_ref[...] = jnp.zeros_like(st_ref)

    z = _dot(wat_ref[...], glrt_ref[...]) + bcol_ref[...]
    la = jax.nn.log_sigmoid(z) * (1.0 / GLA_TAU)
    parts = _split3(la)
    sufm = sufm_ref[...]
    chm = chm_ref[...]
    suf = _dot(parts[0], sufm) + _dot(parts[1], sufm) + _dot(parts[2], sufm)
    tot = _dot(parts[0], chm) + _dot(parts[1], chm) + _dot(parts[2], chm)
    kdec = (gkt_ref[...] * jnp.exp(suf)).astype(BF16)
    dec = jnp.exp(tot)
    gq = gq_ref[...]
    gv = gv_ref[...]
    for c in range(tb // CHUNK):
        fr = slice(c * CHUNK, (c + 1) * CHUNK)
        for hd in range(GLA_HEADS):
            kr = slice(hd * GLA_DK, (hd + 1) * GLA_DK)
            vr = slice(hd * GLA_DV, (hd + 1) * GLA_DV)
            ds = _dot(kdec[kr, fr], gv[fr, vr])
            st = dec[kr, c:c + 1] * st_ref[hd] + ds
            st_ref[hd] = st
            o = _dot(gq[fr, kr], st.astype(BF16))
            o = o * lax.rsqrt(jnp.mean(o * o, axis=-1, keepdims=True) + EPS) * ggn_ref[:, vr]
            y_ref[fr, vr] = (o * jax.nn.silu(gout_ref[fr, vr])).astype(BF16)


def _gla(gq, gkt, gv, glrt, gout, w, B, S, tb):
    T = gq.shape[0]
    nb = S // tb
    nqk = GLA_HEADS * GLA_DK
    nv = GLA_HEADS * GLA_DV
    row = lambda n: pl.BlockSpec((tb, n), lambda b, j: (b * nb + j, 0))
    col = lambda n: pl.BlockSpec((n, tb), lambda b, j: (0, b * nb + j))
    fr = np.arange(tb)
    same = (fr[:, None] // CHUNK) == (fr[None, :] // CHUNK)
    sufm = jnp.asarray(same & (fr[:, None] > fr[None, :]), BF16)
    chm = jnp.asarray((fr[:, None] // CHUNK) == np.arange(LANES)[None, :], BF16)
    consts = [w['w_a2t'], w['b_a2col'], w['g_gla'], sufm, chm]
    return pl.pallas_call(
        functools.partial(_gla_kernel, tb=tb),
        grid=(B, nb),
        in_specs=[row(nqk), col(nqk), row(nv), col(GLA_GATE_RANK), row(nv)] + [_full(a.shape) for a in consts],
        out_specs=row(nv),
        out_shape=jax.ShapeDtypeStruct((T, nv), BF16),
        scratch_shapes=[pltpu.VMEM((GLA_HEADS, GLA_DK, GLA_DV), F32)],
        compiler_params=_params(("parallel", "arbitrary")), name="gla",
    )(gq, gkt, gv, glrt, gout, *consts)


def _merge_kernel(x_ref, ya_ref, yb_ref, br_ref, wa_ref, wb_ref, wo_ref, gffn_ref, wpq_ref, k1_ref, k2_ref, subk2_ref,
                  x1_out, h2t_out, s1_out, s2_out, s2h_out):
    D = x_ref.shape[1]
    gates = jax.nn.sigmoid(br_ref[...])
    merged = gates[:, :D] * _dot(ya_ref[...], wa_ref[...]) + gates[:, D:] * _dot(yb_ref[...], wb_ref[...])
    x1 = x_ref[...] + _dot(merged.astype(BF16), wo_ref[...])
    x1_out[...] = x1
    h2f = _rms(x1, gffn_ref[...])
    h2 = h2f.astype(BF16)
    h2t_out[...] = h2f.T.astype(BF16)
    pq = _dot(h2, wpq_ref[...]).astype(BF16)
    nh = PEER_HEADS * PEER_HALF
    s1_out[...] = _dot_nt(k1_ref[...], pq[:, :nh])
    s2_out[...] = _dot_nt(k2_ref[...], pq[:, nh:])
    for hd in range(PEER_HEADS):
        s2h_out[hd] = _dot_nt(subk2_ref[hd], pq[:, nh + hd * PEER_HALF:nh + (hd + 1) * PEER_HALF])


def _merge(x2, ya, yb, br, w, tm):
    T, D = x2.shape
    row = lambda n: pl.BlockSpec((tm, n), lambda i: (i, 0))
    col = lambda n: pl.BlockSpec((n, tm), lambda i: (0, i))
    nkh = PEER_NKEYS * PEER_HEADS
    consts = [w['w_a'], w['w_b'], w['w_out'], w['g_ffn'], w['w_pq'], w['keys1'], w['keys2'], w['subk2']]
    return pl.pallas_call(
        _merge_kernel,
        grid=(T // tm,),
        in_specs=[row(D), row(ya.shape[1]), row(yb.shape[1]), row(2 * D)] + [_full(a.shape) for a in consts],
        out_specs=[row(D), col(D), col(nkh), col(nkh),
                   pl.BlockSpec((PEER_HEADS, PEER_NKEYS, tm), lambda i: (0, 0, i))],
        out_shape=[jax.ShapeDtypeStruct((T, D), F32), jax.ShapeDtypeStruct((D, T), BF16),
                   jax.ShapeDtypeStruct((nkh, T), F32), jax.ShapeDtypeStruct((nkh, T), F32),
                   jax.ShapeDtypeStruct((PEER_HEADS, PEER_NKEYS, T), F32)],
        compiler_params=_params(("parallel",)), name="merge",
    )(x2, ya, yb, br, *consts)


_NSORT = PEER_TOPK + 1
_CAND = [(i, j) for i in range(_NSORT) for j in range(_NSORT) if (i + 1) * (j + 1) <= _NSORT]


def _insert_sorted(ms, x):
    out = []
    for m in ms:
        out.append(jnp.maximum(m, x))
        x = jnp.minimum(m, x)
    return tuple(out)


def _bitonic_merge(xs):
    n = len(xs)
    if n == 1:
        return xs
    half = n // 2
    hi = [jnp.maximum(xs[i], xs[i + half]) for i in range(half)]
    lo = [jnp.minimum(xs[i], xs[i + half]) for i in range(half)]
    return _bitonic_merge(hi) + _bitonic_merge(lo)


def _bitonic_sort(xs):
    if len(xs) == 1:
        return xs
    half = len(xs) // 2
    return _bitonic_merge(_bitonic_sort(xs[:half]) + _bitonic_sort(xs[half:])[::-1])


def _merge_top(t1, e1, t2, e2):
    k = len(t1)
    hi = [jnp.maximum(t1[i], t2[k - 1 - i]) for i in range(k)]
    lo = [jnp.minimum(t1[i], t2[k - 1 - i]) for i in range(k)]
    return _bitonic_merge(hi), functools.reduce(jnp.maximum, lo + [e1, e2])


def _count_sorted(v, x, strict):
    above = (lambda a: a > x) if strict else (lambda a: a >= x)
    c = [above(v[4 * i + 3]) for i in range(4)]
    pick = lambda j: jnp.where(c[2], v[12 + j], jnp.where(c[1], v[8 + j], jnp.where(c[0], v[4 + j], v[j])))
    fine = jnp.where(above(pick(2)), 3.0, jnp.where(above(pick(1)), 2.0, jnp.where(above(pick(0)), 1.0, 0.0)))
    base = jnp.where(c[2], 12.0, jnp.where(c[1], 8.0, jnp.where(c[0], 4.0, 0.0)))
    return jnp.where(c[3], 16.0, base + fine)


_KEY_ROWS = 16
_RANK_ROWS = 64


def _topk_kernel(s1_ref, s2_ref, s2h_ref, cnt_out, p1_out, rank_out, p2_out,
                 thr_ref, top1_ref, zinv_ref, b_ref, mid_ref, *, tl):
    neg = jnp.full((PEER_HEADS, LANES), -jnp.inf, F32)

    def largest(s_ref, ln):
        def summary(lo, hi):
            if hi - lo == PEER_TOPK:
                return _bitonic_sort([s_ref[n, :, ln] for n in range(lo, hi)]), neg
            mid = (lo + hi) // 2
            return _merge_top(*summary(lo, mid), *summary(mid, hi))

        top, nxt = summary(0, PEER_NKEYS)
        return tuple(top) + (nxt,)

    for c0 in range(0, tl, LANES):
        ln = pl.ds(c0, LANES)
        a = largest(s1_ref, ln)
        b = largest(s2_ref, ln)
        cands = [a[i] + b[j] for (i, j) in _CAND]
        best = (neg,) * _NSORT
        for cd in cands:
            best = _insert_sorted(best, cd)
        thr = 0.5 * (best[PEER_TOPK - 1] + best[PEER_TOPK])
        top = a[0] + b[0]
        z = functools.reduce(lambda u, v: u + v, [jnp.where(cd >= thr, jnp.exp(cd - top), 0.0) for cd in cands])
        thr_ref[:, ln] = thr
        top1_ref[:, ln] = a[0]
        zinv_ref[:, ln] = 0.5 / z
        for kk in range(PEER_TOPK):
            b_ref[kk, :, ln] = b[kk]
            mid_ref[kk, :, ln] = 0.5 * (b[kk] + b[kk + 1])

    def count_rows(c, _):
        rows = pl.ds(pl.multiple_of(c * _KEY_ROWS, _KEY_ROWS), _KEY_ROWS)
        s1 = s1_ref[rows]
        tau = thr_ref[...][None] - s1
        cnt_out[rows] = _count_sorted([b_ref[kk][None] for kk in range(PEER_TOPK)], tau, strict=False)
        p1_out[rows] = jnp.exp(s1 - top1_ref[...][None])
        return 0

    lax.fori_loop(0, PEER_NKEYS // _KEY_ROWS, count_rows, 0)

    for hd in range(PEER_HEADS):
        def rank_rows(c, _, hd=hd):
            rows = pl.ds(pl.multiple_of(c * _RANK_ROWS, _RANK_ROWS), _RANK_ROWS)
            s2 = s2h_ref[hd, rows, :]
            mids = [mid_ref[kk, hd:hd + 1, :] for kk in range(PEER_TOPK)]
            rank_out[hd, rows, :] = _count_sorted(mids, s2, strict=True)
            p2_out[hd, rows, :] = jnp.exp(s2 - b_ref[0, hd:hd + 1, :]) * zinv_ref[hd:hd + 1, :]
            return 0

        lax.fori_loop(0, PEER_NKEYS // _RANK_ROWS, rank_rows, 0)


def _topk(s1, s2, s2h, tl):
    T = s2h.shape[2]
    nh_spec = pl.BlockSpec((PEER_NKEYS, PEER_HEADS, tl), lambda i: (0, 0, i))
    hn_spec = pl.BlockSpec((PEER_HEADS, PEER_NKEYS, tl), lambda i: (0, 0, i))
    nh_shape = jax.ShapeDtypeStruct((PEER_NKEYS, PEER_HEADS, T), F32)
    hn_shape = jax.ShapeDtypeStruct((PEER_HEADS, PEER_NKEYS, T), F32)
    return pl.pallas_call(
        functools.partial(_topk_kernel, tl=tl),
        grid=(T // tl,),
        in_specs=[nh_spec, nh_spec, hn_spec],
        out_specs=[nh_spec, nh_spec, hn_spec, hn_spec], out_shape=[nh_shape, nh_shape, hn_shape, hn_shape],
        scratch_shapes=[pltpu.VMEM((PEER_HEADS, tl), F32) for _ in range(3)]
                       + [pltpu.VMEM((PEER_TOPK, PEER_HEADS, tl), F32) for _ in range(2)],
        compiler_params=_params(("parallel",)), name="topk",
    )(s1, s2, s2h)


_ROWS = 32
_PACK = 16
_MM_ROWS = 512


def _experts_kernel(h2t_ref, u_ref, vt_ref, rank_ref, cnt_ref, p1_ref, p2_ref, x1_ref, gfin_ref,
                    out_ref, acc_ref, at_ref, ct_ref, cb_ref, pb_ref, *, gi, ne):
    j = pl.program_id(1)
    tm = h2t_ref.shape[1]

    @pl.when(j == 0)
    def _():
        acc_ref[...] = jnp.zeros_like(acc_ref)
        at_ref[...] = jnp.zeros_like(at_ref)
        ct_ref[...] = jnp.zeros_like(ct_ref)

    gate_valid = jnp.logical_and(j >= 1, j <= ne)
    jb = jnp.clip(j - 1, 0, ne - 1)

    def stages(slot):
        other = 1 - slot

        def value_rows(r0):
            rows = pl.ds(r0, _MM_ROWS)
            acc_ref[rows, :] += _dot(vt_ref[rows, :], ct_ref[slot])

        def preact_rows(r0):
            rows = pl.ds(r0, _MM_ROWS)
            at_ref[slot, rows, :] = _dot(u_ref[rows, :], h2t_ref[...])

        def column_rows(c0):
            ci = c0 // LANES
            ln = pl.ds(c0, LANES)
            for g in range(gi):
                cnt = jnp.where(gate_valid, cnt_ref[jb * gi + g, :, ln], 0.0)
                p1 = p1_ref[jb * gi + g, :, ln]
                for hd in range(PEER_HEADS):
                    cb_ref[ci, g * PEER_HEADS + hd] = jnp.broadcast_to(cnt[hd:hd + 1], (_PACK, LANES)).astype(BF16)
                    pb_ref[ci, g * PEER_HEADS + hd] = jnp.broadcast_to(p1[hd:hd + 1], (_PACK, LANES)).astype(BF16)

        def gate_chunk(c0, r0):
            if r0 == 0:
                column_rows(c0)
            ci = c0 // LANES
            ln = pl.ds(c0, LANES)
            subs = range(r0, r0 + _ROWS, _PACK)
            wgt = [[None] * len(subs) for _ in range(gi)]
            for hd in range(PEER_HEADS):
                rank = [rank_ref[hd, pl.ds(r, _PACK), ln].astype(BF16) for r in subs]
                p2 = [p2_ref[hd, pl.ds(r, _PACK), ln].astype(BF16) for r in subs]
                for g in range(gi):
                    cnt = cb_ref[ci, g * PEER_HEADS + hd]
                    p1 = pb_ref[ci, g * PEER_HEADS + hd]
                    for k in range(len(subs)):
                        term = jnp.where(rank[k] < cnt, p2[k], jnp.zeros_like(p2[k])) * p1
                        wgt[g][k] = term if wgt[g][k] is None else wgt[g][k] + term
            for g in range(gi):
                for k, r in enumerate(subs):
                    er = pl.ds(g * PEER_NKEYS + r, _PACK)
                    a = at_ref[other, er, ln]
                    act = a * (1.0 + lax.erf(a * (2.0 ** -0.5)))
                    ct_ref[other, er, ln] = act.astype(BF16) * wgt[g][k]

        mxu_tasks = ([functools.partial(value_rows, r0) for r0 in range(0, acc_ref.shape[0], _MM_ROWS)]
                     + [functools.partial(preact_rows, r0) for r0 in range(0, u_ref.shape[0], _MM_ROWS)])
        vpu_tasks = [functools.partial(gate_chunk, c0, r0)
                     for c0 in range(0, tm, LANES) for r0 in range(0, PEER_NKEYS, _ROWS)]
        done = 0
        for n, task in enumerate(vpu_tasks):
            want = ((n + 1) * len(mxu_tasks)) // len(vpu_tasks)
            while done < want:
                mxu_tasks[done]()
                done += 1
            task()

    @pl.when(j % 2 == 0)
    def _():
        stages(0)

    @pl.when(j % 2 == 1)
    def _():
        stages(1)

    @pl.when(j == pl.num_programs(1) - 1)
    def _():
        x = x1_ref[...] + acc_ref[...].T
        out_ref[...] = _rms(x, gfin_ref[...])


def _experts(h2t, u, vt, rank, cnt, p1, p2, x1, gfin, tm, te):
    T, D = x1.shape
    E = u.shape[0]
    gi = te // PEER_NKEYS
    ne = E // te
    tok = lambda n: pl.BlockSpec((tm, n), lambda i, j: (i, 0))
    lanes = lambda a: pl.BlockSpec((a.shape[0], a.shape[1], tm), lambda i, j: (0, 0, i))
    row_tables = pltpu.VMEM((tm // LANES, gi * PEER_HEADS, _PACK, LANES), BF16)
    return pl.pallas_call(
        functools.partial(_experts_kernel, gi=gi, ne=ne),
        grid=(T // tm, ne + 2),
        in_specs=[pl.BlockSpec((D, tm), lambda i, j: (0, i)),
                  pl.BlockSpec((te, D), lambda i, j: (jnp.minimum(j, ne - 1), 0)),
                  pl.BlockSpec((D, te), lambda i, j: (0, jnp.clip(j - 2, 0, ne - 1))),
                  lanes(rank), lanes(cnt), lanes(p1), lanes(p2), tok(D), _full(gfin.shape)],
        out_specs=tok(D),
        out_shape=jax.ShapeDtypeStruct((T, D), F32),
        scratch_shapes=[pltpu.VMEM((D, tm), F32), pltpu.VMEM((2, te, tm), F32), pltpu.VMEM((2, te, tm), BF16),
                        row_tables, row_tables],
        compiler_params=_params(("parallel", "arbitrary")), name="experts",
    )(h2t, u, vt, rank, cnt, p1, p2, x1, gfin)


def _rot_half_cols(w):
    half = w.shape[-1] // 2
    return jnp.concatenate([-w[..., half:], w[..., :half]], axis=-1)


def _prep_weights(g_mix, w_in, g_q_lat, w_qb, g_kv_lat, w_kvb, w_a2, b_a2, g_gla, w_branch_a, w_branch_b,
                  w_out, g_ffn, w_peer_q, peer_sub_keys):
    D = w_in.shape[0]
    sizes = (MLA_Q_RANK, MLA_KV_RANK, MLA_ROPE, GLA_HEADS * GLA_DK, GLA_HEADS * GLA_DK, GLA_HEADS * GLA_DV,
             GLA_GATE_RANK, GLA_HEADS * GLA_DV, 2 * D)
    pts = [int(s) for s in np.cumsum(sizes)[:-1]]
    w_ql, w_kvl, w_kr, w_gq, w_gk, w_gv, w_glr, w_go, w_br = jnp.split(w_in, pts, axis=1)
    H = MLA_HEADS
    zpad = lambda *shape: jnp.zeros(shape, F32)
    tail = HEAD_LANES - MLA_NOPE - MLA_ROPE

    def rope_block(wr):
        lead = wr.shape[:-1]
        return jnp.concatenate([zpad(*lead, MLA_NOPE), wr, zpad(*lead, tail)], -1)

    wq3 = w_qb.reshape(MLA_Q_RANK, H, MLA_NOPE + MLA_ROPE)
    q_rope = wq3[..., MLA_NOPE:]
    q0 = jnp.concatenate([wq3[..., :MLA_NOPE], q_rope, zpad(MLA_Q_RANK, H, tail)], -1)
    q1 = rope_block(_rot_half_cols(q_rope))
    wkv3 = w_kvb.reshape(MLA_KV_RANK, H, MLA_NOPE + MLA_V)
    wk = jnp.concatenate([wkv3[..., :MLA_NOPE], zpad(MLA_KV_RANK, H, HEAD_LANES - MLA_NOPE)], -1)
    wv = jnp.concatenate([wkv3[..., MLA_NOPE:], zpad(MLA_KV_RANK, H, HEAD_LANES - MLA_V)], -1)
    wa3 = w_branch_a.reshape(H, MLA_V, D)
    wa = jnp.concatenate([wa3, zpad(H, HEAD_LANES - MLA_V, D)], axis=1).reshape(H * HEAD_LANES, D)

    PH = PEER_HEADS
    wpq = w_peer_q.reshape(D, PH, 2, PEER_HALF).transpose(0, 2, 1, 3).reshape(D, 2 * PH * PEER_HALF)
    eye = jnp.eye(PH, dtype=F32)

    def keys_by_key_head(sk):
        return jnp.einsum('hg,hnd->nhgd', eye, sk).reshape(PEER_NKEYS * PH, PH * PEER_HALF)

    return {
        'g_mix': g_mix.reshape(1, D),
        'w_lat': jnp.concatenate([w_ql, w_kvl], 1).astype(BF16),
        'w_kr': jnp.concatenate([rope_block(w_kr), rope_block(_rot_half_cols(w_kr))], 1).astype(BF16),
        'w_g': jnp.concatenate([w_gq, w_gv, w_go], 1).astype(BF16),
        'w_gt': jnp.concatenate([w_gk, w_glr], 1).T.astype(BF16),
        'w_br': w_br.astype(BF16),
        'g_q_lat': g_q_lat.reshape(1, -1), 'g_kv_lat': g_kv_lat.reshape(1, -1),
        'w_q': q0.reshape(MLA_Q_RANK, H * HEAD_LANES).astype(BF16),
        'w_qr': q1.reshape(MLA_Q_RANK, H * HEAD_LANES).astype(BF16),
        'w_k': wk.reshape(MLA_KV_RANK, H * HEAD_LANES).astype(BF16),
        'w_v': wv.reshape(MLA_KV_RANK, H * HEAD_LANES).astype(BF16),
        'w_a2t': w_a2.T.astype(BF16), 'b_a2col': b_a2.reshape(-1, 1), 'g_gla': g_gla.reshape(1, -1),
        'w_a': wa.astype(BF16), 'w_b': w_branch_b.astype(BF16), 'w_out': w_out.astype(BF16),
        'g_ffn': g_ffn.reshape(1, D), 'w_pq': wpq.astype(BF16),
        'keys1': keys_by_key_head(peer_sub_keys[:, 0]).astype(BF16),
        'keys2': keys_by_key_head(peer_sub_keys[:, 1]).astype(BF16),
        'subk2': peer_sub_keys[:, 1].astype(BF16),
    }


def _rope_freq_lanes():
    half = MLA_ROPE // 2
    inv = ROPE_THETA ** (-np.arange(half, dtype=np.float32) / half)
    lanes = np.zeros((1, HEAD_LANES), np.float32)
    lanes[0, MLA_NOPE:MLA_NOPE + half] = inv
    lanes[0, MLA_NOPE + half:MLA_NOPE + 2 * half] = inv
    return jnp.asarray(lanes)


def _block(x, positions, w, peer_u, peer_v, g_final, *, tm_proj, tq, nsub, tb, tm_merge, tl, tm_exp, te):
    B, S, D = x.shape
    T = B * S
    x2 = x.reshape(T, D)
    posf = positions.reshape(T, 1).astype(F32)
    q, k, v, gq, gkt, gv, glrt, gout, br = _proj(x2, posf, _rope_freq_lanes(), w, tm_proj)
    ya = _attn(q, k, v, B, S, tq, nsub)
    yb = _gla(gq, gkt, gv, glrt, gout, w, B, S, tb)
    x1, h2t, s1, s2, s2h = _merge(x2, ya, yb, br, w, tm_merge)
    s1 = s1.reshape(PEER_NKEYS, PEER_HEADS, T)
    s2 = s2.reshape(PEER_NKEYS, PEER_HEADS, T)
    cnt, p1, rank, p2 = _topk(s1, s2, s2h, tl)
    out = _experts(h2t, peer_u.astype(BF16), peer_v.T.astype(BF16), rank, cnt, p1, p2, x1,
                   g_final.reshape(1, D), tm_exp, te)
    return out.reshape(B, S, D)


def kernel(x, positions, g_mix, w_in, g_q_lat, w_qb, g_kv_lat, w_kvb, w_a2, b_a2, g_gla, w_branch_a, w_branch_b, w_out, g_ffn, w_peer_q, peer_sub_keys, peer_u, peer_v, g_final):
    assert g_mix.shape[0] == 1, "single-layer block"
    w = _prep_weights(g_mix[0], w_in[0], g_q_lat[0], w_qb[0], g_kv_lat[0], w_kvb[0], w_a2[0], b_a2[0], g_gla[0],
                      w_branch_a[0], w_branch_b[0], w_out[0], g_ffn[0], w_peer_q[0], peer_sub_keys[0])
    return _block(x, positions, w, peer_u[0], peer_v[0], g_final,
                  tm_proj=512, tq=1024, nsub=2, tb=512, tm_merge=512, tl=256, tm_exp=512, te=1024)
```

```python
import functools
import math

import numpy as np
import jax
import jax.numpy as jnp
from jax import lax
from jax.experimental import pallas as pl
from jax.experimental.pallas import tpu as pltpu

F32 = jnp.float32
BF16 = jnp.bfloat16

EPS = 1e-6
ROPE_THETA = 10000.0
CHUNK = 64

MLA_HEADS = 8
MLA_Q_RANK = 384
MLA_KV_RANK = 256
MLA_NOPE = 64
MLA_ROPE = 32
MLA_V = 64
HEAD_LANES = 128

GLA_HEADS = 4
GLA_DK = 64
GLA_DV = 128
GLA_GATE_RANK = 16
GLA_TAU = 16.0

PEER_HEADS = 8
PEER_NKEYS = 128
PEER_HALF = 128
PEER_TOPK = 16

LANES = 128
VMEM_LIMIT_BYTES = 56 * 1024 * 1024

_NT = (((1,), (1,)), ((), ()))


def _dot(a, b):
    return jnp.dot(a, b, preferred_element_type=F32)


def _dot_nt(a, b):
    return lax.dot_general(a, b, _NT, preferred_element_type=F32)


def _rms(x, g):
    return x * lax.rsqrt(jnp.mean(x * x, axis=-1, keepdims=True) + EPS) * g


def _params(sem):
    return pltpu.CompilerParams(dimension_semantics=sem, vmem_limit_bytes=VMEM_LIMIT_BYTES)


def _full(shape):
    nd = len(shape)
    return pl.BlockSpec(shape, lambda *_: (0,) * nd, pipeline_mode=pl.Buffered(1))


def _proj_kernel(x_ref, pos_ref, freq_ref, gmix_ref, wlat_ref, wkr_ref, wg_ref, wgt_ref, wbr_ref,
                 gql_ref, wq_ref, wqr_ref, gkv_ref, wk_ref, wv_ref,
                 q_out, k_out, v_out, gq_out, gkt_out, gv_out, glrt_out, gout_out, br_out, *, q_scale):
    h = _rms(x_ref[...], gmix_ref[...]).astype(BF16)
    ang = pos_ref[...] * freq_ref[...]
    cos = jnp.cos(ang)
    sin = jnp.sin(ang)

    lat = _dot(h, wlat_ref[...])
    qn = _rms(lat[:, :MLA_Q_RANK], gql_ref[...]).astype(BF16)
    kvn = _rms(lat[:, MLA_Q_RANK:], gkv_ref[...]).astype(BF16)
    q0 = _dot(qn, wq_ref[...])
    q1 = _dot(qn, wqr_ref[...])
    kr = _dot(h, wkr_ref[...])
    kpe = kr[:, :HEAD_LANES] * cos + kr[:, HEAD_LANES:] * sin
    kk = _dot(kvn, wk_ref[...])
    for hd in range(MLA_HEADS):
        sl = slice(hd * HEAD_LANES, (hd + 1) * HEAD_LANES)
        q_out[:, sl] = ((q0[:, sl] * cos + q1[:, sl] * sin) * q_scale).astype(BF16)
        k_out[:, sl] = (kk[:, sl] + kpe).astype(BF16)
    ones_lane = lax.broadcasted_iota(jnp.int32, (1, MLA_HEADS * HEAD_LANES), 1) % HEAD_LANES == MLA_V
    v_out[...] = (_dot(kvn, wv_ref[...]) + ones_lane.astype(F32)).astype(BF16)

    nqk = GLA_HEADS * GLA_DK
    nv = GLA_HEADS * GLA_DV
    g = _dot(h, wg_ref[...])
    gq_out[...] = (g[:, :nqk] * (GLA_DK ** -0.5)).astype(BF16)
    gv_out[...] = g[:, nqk:nqk + nv].astype(BF16)
    gout_out[...] = g[:, nqk + nv:]
    gt = _dot_nt(wgt_ref[...], h)
    gkt_out[...] = gt[:nqk]
    glrt_out[...] = gt[nqk:].astype(BF16)
    br_out[...] = _dot(h, wbr_ref[...])


def _proj(x2, posf, freq, w, tm):
    T, D = x2.shape
    nqk = GLA_HEADS * GLA_DK
    nv = GLA_HEADS * GLA_DV
    hl = MLA_HEADS * HEAD_LANES
    row = lambda n: pl.BlockSpec((tm, n), lambda i: (i, 0))
    col = lambda n: pl.BlockSpec((n, tm), lambda i: (0, i))
    ins = [x2, posf, freq, w['g_mix'], w['w_lat'], w['w_kr'], w['w_g'], w['w_gt'], w['w_br'],
           w['g_q_lat'], w['w_q'], w['w_qr'], w['g_kv_lat'], w['w_k'], w['w_v']]
    in_specs = [row(D), row(1)] + [_full(a.shape) for a in ins[2:]]
    out_shape = [
        jax.ShapeDtypeStruct((T, hl), BF16), jax.ShapeDtypeStruct((T, hl), BF16),
        jax.ShapeDtypeStruct((T, hl), BF16),
        jax.ShapeDtypeStruct((T, nqk), BF16), jax.ShapeDtypeStruct((nqk, T), F32),
        jax.ShapeDtypeStruct((T, nv), BF16), jax.ShapeDtypeStruct((GLA_GATE_RANK, T), BF16),
        jax.ShapeDtypeStruct((T, nv), F32), jax.ShapeDtypeStruct((T, 2 * D), F32),
    ]
    out_specs = [row(hl), row(hl), row(hl), row(nqk), col(nqk), row(nv), col(GLA_GATE_RANK),
                 row(nv), row(2 * D)]
    q_scale = (MLA_NOPE + MLA_ROPE) ** -0.5 * math.log2(math.e)
    return pl.pallas_call(
        functools.partial(_proj_kernel, q_scale=q_scale),
        grid=(T // tm,), in_specs=in_specs, out_specs=out_specs, out_shape=out_shape,
        compiler_params=_params(("parallel",)), name="proj",
    )(*ins)


def _attn_kernel(q_ref, k_ref, v_ref, o_ref, sa_ref, sb_ref, *, tq, nsub):
    i = pl.program_id(2)
    qs = [q_ref[pl.ds(s * tq, tq), :] for s in range(nsub)]

    def rows(ref, j):
        return ref[pl.ds(pl.multiple_of(j * tq, tq), tq), :]

    def scores(j, dst, subs):
        kj = rows(k_ref, j)
        for s in subs:
            dst[s] = _dot_nt(qs[s], kj)

    def update(carry, s, vj, masked):
        m, acc = carry
        if masked:
            qc = lax.broadcasted_iota(jnp.int32, (tq, tq), 0) // CHUNK
            kc = lax.broadcasted_iota(jnp.int32, (tq, tq), 1) // CHUNK
            s = jnp.where(kc <= qc, s, -1e30)
        m_new = jnp.maximum(m, jnp.max(s, axis=-1, keepdims=True))
        p = jnp.exp2(s - m_new)
        acc = jnp.exp2(m - m_new) * acc + _dot(p.astype(BF16), vj)
        return m_new, acc

    def phase(j, carry, cur, nxt):
        scores(j + 1, nxt, range(nsub))
        vj = rows(v_ref, j)
        return tuple(update(carry[s], cur[s], vj, False) for s in range(nsub))

    def body(jj, carry):
        carry = phase(2 * jj, carry, sa_ref, sb_ref)
        return phase(2 * jj + 1, carry, sb_ref, sa_ref)

    init = tuple((jnp.full((tq, 1), -1e30, F32), jnp.zeros((tq, HEAD_LANES), F32)) for _ in range(nsub))
    scores(0, sa_ref, range(nsub))
    carry = list(lax.fori_loop(0, i, body, init))
    scores(2 * i + 1, sb_ref, [1])
    v0 = rows(v_ref, 2 * i)
    carry[0] = update(carry[0], sa_ref[0], v0, True)
    carry[1] = update(carry[1], sa_ref[1], v0, False)
    carry[1] = update(carry[1], sb_ref[1], rows(v_ref, 2 * i + 1), True)
    lane = lax.broadcasted_iota(jnp.int32, (tq, HEAD_LANES), 1)
    for s in range(nsub):
        acc = carry[s][1]
        o = acc / acc[:, MLA_V:MLA_V + 1]
        o_ref[pl.ds(s * tq, tq), :] = jnp.where(lane < MLA_V, o, 0.0).astype(BF16)


def _attn(q, k, v, B, S, tq, nsub):
    assert nsub == 2, "the score double-buffering is written for two query sub-tiles per step"
    T = q.shape[0]
    nq = S // (tq * nsub)
    qspec = pl.BlockSpec((tq * nsub, HEAD_LANES), lambda b, h, i: (b * nq + i, h))
    kvspec = pl.BlockSpec((S, HEAD_LANES), lambda b, h, i: (b, h))
    return pl.pallas_call(
        functools.partial(_attn_kernel, tq=tq, nsub=nsub),
        grid=(B, MLA_HEADS, nq), in_specs=[qspec, kvspec, kvspec], out_specs=qspec,
        out_shape=jax.ShapeDtypeStruct((T, MLA_HEADS * HEAD_LANES), BF16),
        scratch_shapes=[pltpu.VMEM((nsub, tq, tq), F32), pltpu.VMEM((nsub, tq, tq), F32)],
        compiler_params=_params(("parallel", "parallel", "arbitrary")), name="attn",
    )(q, k, v)


def _split3(x):
    hi = x.astype(BF16)
    r = x - hi.astype(F32)
    mid = r.astype(BF16)
    lo = (r - mid.astype(F32)).astype(BF16)
    return hi, mid, lo


def _gla_kernel(gq_ref, gkt_ref, gv_ref, glrt_ref, gout_ref, wat_ref, bcol_ref, ggn_ref, sufm_ref, chm_ref,
                y_ref, st_ref, *, tb):
    @pl.when(pl.program_id(1) == 0)
    def _():
        st_ref[...] = jnp.zeros_like(st_ref)

    z = _dot(wat_ref[...], glrt_ref[...]) + bcol_ref[...]
    la = jax.nn.log_sigmoid(z) * (1.0 / GLA_TAU)
    parts = _split3(la)
    sufm = sufm_ref[...]
    chm = chm_ref[...]
    suf = _dot(parts[0], sufm) + _dot(parts[1], sufm) + _dot(parts[2], sufm)
    tot = _dot(parts[0], chm) + _dot(parts[1], chm) + _dot(parts[2], chm)
    kdec = (gkt_ref[...] * jnp.exp(suf)).astype(BF16)
    dec = jnp.exp(tot)
    gq = gq_ref[...]
    gv = gv_ref[...]
    for c in range(tb // CHUNK):
        fr = slice(c * CHUNK, (c + 1) * CHUNK)
        for hd in range(GLA_HEADS):
            kr = slice(hd * GLA_DK, (hd + 1) * GLA_DK)
            vr = slice(hd * GLA_DV, (hd + 1) * GLA_DV)
            ds = _dot(kdec[kr, fr], gv[fr, vr])
            st = dec[kr, c:c + 1] * st_ref[hd] + ds
            st_ref[hd] = st
            o = _dot(gq[fr, kr], st.astype(BF16))
            o = o * lax.rsqrt(jnp.mean(o * o, axis=-1, keepdims=True) + EPS) * ggn_ref[:, vr]
            y_ref[fr, vr] = (o * jax.nn.silu(gout_ref[fr, vr])).astype(BF16)


def _gla(gq, gkt, gv, glrt, gout, w, B, S, tb):
    T = gq.shape[0]
    nb = S // tb
    nqk = GLA_HEADS * GLA_DK
    nv = GLA_HEADS * GLA_DV
    row = lambda n: pl.BlockSpec((tb, n), lambda b, j: (b * nb + j, 0))
    col = lambda n: pl.BlockSpec((n, tb), lambda b, j: (0, b * nb + j))
    fr = np.arange(tb)
    same = (fr[:, None] // CHUNK) == (fr[None, :] // CHUNK)
    sufm = jnp.asarray(same & (fr[:, None] > fr[None, :]), BF16)
    chm = jnp.asarray((fr[:, None] // CHUNK) == np.arange(LANES)[None, :], BF16)
    consts = [w['w_a2t'], w['b_a2col'], w['g_gla'], sufm, chm]
    return pl.pallas_call(
        functools.partial(_gla_kernel, tb=tb),
        grid=(B, nb),
        in_specs=[row(nqk), col(nqk), row(nv), col(GLA_GATE_RANK), row(nv)] + [_full(a.shape) for a in consts],
        out_specs=row(nv),
        out_shape=jax.ShapeDtypeStruct((T, nv), BF16),
        scratch_shapes=[pltpu.VMEM((GLA_HEADS, GLA_DK, GLA_DV), F32)],
        compiler_params=_params(("parallel", "arbitrary")), name="gla",
    )(gq, gkt, gv, glrt, gout, *consts)


def _merge_kernel(x_ref, ya_ref, yb_ref, br_ref, wa_ref, wb_ref, wo_ref, gffn_ref, wpq_ref, k1_ref, k2_ref, subk2_ref,
                  x1_out, h2t_out, s1_out, s2_out, s2h_out):
    D = x_ref.shape[1]
    gates = jax.nn.sigmoid(br_ref[...])
    merged = gates[:, :D] * _dot(ya_ref[...], wa_ref[...]) + gates[:, D:] * _dot(yb_ref[...], wb_ref[...])
    x1 = x_ref[...] + _dot(merged.astype(BF16), wo_ref[...])
    x1_out[...] = x1
    h2f = _rms(x1, gffn_ref[...])
    h2 = h2f.astype(BF16)
    h2t_out[...] = h2f.T.astype(BF16)
    pq = _dot(h2, wpq_ref[...]).astype(BF16)
    nh = PEER_HEADS * PEER_HALF
    s1_out[...] = _dot_nt(k1_ref[...], pq[:, :nh])
    s2_out[...] = _dot_nt(k2_ref[...], pq[:, nh:])
    for hd in range(PEER_HEADS):
        s2h_out[hd] = _dot_nt(subk2_ref[hd], pq[:, nh + hd * PEER_HALF:nh + (hd + 1) * PEER_HALF])


def _merge(x2, ya, yb, br, w, tm):
    T, D = x2.shape
    row = lambda n: pl.BlockSpec((tm, n), lambda i: (i, 0))
    col = lambda n: pl.BlockSpec((n, tm), lambda i: (0, i))
    nkh = PEER_NKEYS * PEER_HEADS
    consts = [w['w_a'], w['w_b'], w['w_out'], w['g_ffn'], w['w_pq'], w['keys1'], w['keys2'], w['subk2']]
    return pl.pallas_call(
        _merge_kernel,
        grid=(T // tm,),
        in_specs=[row(D), row(ya.shape[1]), row(yb.shape[1]), row(2 * D)] + [_full(a.shape) for a in consts],
        out_specs=[row(D), col(D), col(nkh), col(nkh),
                   pl.BlockSpec((PEER_HEADS, PEER_NKEYS, tm), lambda i: (0, 0, i))],
        out_shape=[jax.ShapeDtypeStruct((T, D), F32), jax.ShapeDtypeStruct((D, T), BF16),
                   jax.ShapeDtypeStruct((nkh, T), F32), jax.ShapeDtypeStruct((nkh, T), F32),
                   jax.ShapeDtypeStruct((PEER_HEADS, PEER_NKEYS, T), F32)],
        compiler_params=_params(("parallel",)), name="merge",
    )(x2, ya, yb, br, *consts)


_NSORT = PEER_TOPK + 1
_CAND = [(i, j) for i in range(_NSORT) for j in range(_NSORT) if (i + 1) * (j + 1) <= _NSORT]


def _insert_sorted(ms, x):
    out = []
    for m in ms:
        out.append(jnp.maximum(m, x))
        x = jnp.minimum(m, x)
    return tuple(out)


def _bitonic_merge(xs):
    n = len(xs)
    if n == 1:
        return xs
    half = n // 2
    hi = [jnp.maximum(xs[i], xs[i + half]) for i in range(half)]
    lo = [jnp.minimum(xs[i], xs[i + half]) for i in range(half)]
    return _bitonic_merge(hi) + _bitonic_merge(lo)


def _bitonic_sort(xs):
    if len(xs) == 1:
        return xs
    half = len(xs) // 2
    return _bitonic_merge(_bitonic_sort(xs[:half]) + _bitonic_sort(xs[half:])[::-1])


def _merge_top(t1, e1, t2, e2):
    k = len(t1)
    hi = [jnp.maximum(t1[i], t2[k - 1 - i]) for i in range(k)]
    lo = [jnp.minimum(t1[i], t2[k - 1 - i]) for i in range(k)]
    return _bitonic_merge(hi), functools.reduce(jnp.maximum, lo + [e1, e2])


def _count_sorted(v, x, strict):
    above = (lambda a: a > x) if strict else (lambda a: a >= x)
    c = [above(v[4 * i + 3]) for i in range(4)]
    pick = lambda j: jnp.where(c[2], v[12 + j], jnp.where(c[1], v[8 + j], jnp.where(c[0], v[4 + j], v[j])))
    fine = jnp.where(above(pick(2)), 3.0, jnp.where(above(pick(1)), 2.0, jnp.where(above(pick(0)), 1.0, 0.0)))
    base = jnp.where(c[2], 12.0, jnp.where(c[1], 8.0, jnp.where(c[0], 4.0, 0.0)))
    return jnp.where(c[3], 16.0, base + fine)


_KEY_ROWS = 16
_RANK_ROWS = 64


def _topk_kernel(s1_ref, s2_ref, s2h_ref, cnt_out, p1_out, rank_out, p2_out,
                 thr_ref, top1_ref, zinv_ref, b_ref, mid_ref, *, tl):
    neg = jnp.full((PEER_HEADS, LANES), -jnp.inf, F32)

    def largest(s_ref, ln):
        def summary(lo, hi):
            if hi - lo == PEER_TOPK:
                return _bitonic_sort([s_ref[n, :, ln] for n in range(lo, hi)]), neg
            mid = (lo + hi) // 2
            return _merge_top(*summary(lo, mid), *summary(mid, hi))

        top, nxt = summary(0, PEER_NKEYS)
        return tuple(top) + (nxt,)

    for c0 in range(0, tl, LANES):
        ln = pl.ds(c0, LANES)
        a = largest(s1_ref, ln)
        b = largest(s2_ref, ln)
        cands = [a[i] + b[j] for (i, j) in _CAND]
        best = (neg,) * _NSORT
        for cd in cands:
            best = _insert_sorted(best, cd)
        thr = 0.5 * (best[PEER_TOPK - 1] + best[PEER_TOPK])
        top = a[0] + b[0]
        z = functools.reduce(lambda u, v: u + v, [jnp.where(cd >= thr, jnp.exp(cd - top), 0.0) for cd in cands])
        thr_ref[:, ln] = thr
        top1_ref[:, ln] = a[0]
        zinv_ref[:, ln] = 0.5 / z
        for kk in range(PEER_TOPK):
            b_ref[kk, :, ln] = b[kk]
            mid_ref[kk, :, ln] = 0.5 * (b[kk] + b[kk + 1])

    def count_rows(c, _):
        rows = pl.ds(pl.multiple_of(c * _KEY_ROWS, _KEY_ROWS), _KEY_ROWS)
        s1 = s1_ref[rows]
        tau = thr_ref[...][None] - s1
        cnt_out[rows] = _count_sorted([b_ref[kk][None] for kk in range(PEER_TOPK)], tau, strict=False)
        p1_out[rows] = jnp.exp(s1 - top1_ref[...][None])
        return 0

    lax.fori_loop(0, PEER_NKEYS // _KEY_ROWS, count_rows, 0)

    for hd in range(PEER_HEADS):
        def rank_rows(c, _, hd=hd):
            rows = pl.ds(pl.multiple_of(c * _RANK_ROWS, _RANK_ROWS), _RANK_ROWS)
            s2 = s2h_ref[hd, rows, :]
            mids = [mid_ref[kk, hd:hd + 1, :] for kk in range(PEER_TOPK)]
            rank_out[hd, rows, :] = _count_sorted(mids, s2, strict=True)
            p2_out[hd, rows, :] = jnp.exp(s2 - b_ref[0, hd:hd + 1, :]) * zinv_ref[hd:hd + 1, :]
            return 0

        lax.fori_loop(0, PEER_NKEYS // _RANK_ROWS, rank_rows, 0)


def _topk(s1, s2, s2h, tl):
    T = s2h.shape[2]
    nh_spec = pl.BlockSpec((PEER_NKEYS, PEER_HEADS, tl), lambda i: (0, 0, i))
    hn_spec = pl.BlockSpec((PEER_HEADS, PEER_NKEYS, tl), lambda i: (0, 0, i))
    nh_shape = jax.ShapeDtypeStruct((PEER_NKEYS, PEER_HEADS, T), F32)
    hn_shape = jax.ShapeDtypeStruct((PEER_HEADS, PEER_NKEYS, T), F32)
    return pl.pallas_call(
        functools.partial(_topk_kernel, tl=tl),
        grid=(T // tl,),
        in_specs=[nh_spec, nh_spec, hn_spec],
        out_specs=[nh_spec, nh_spec, hn_spec, hn_spec], out_shape=[nh_shape, nh_shape, hn_shape, hn_shape],
        scratch_shapes=[pltpu.VMEM((PEER_HEADS, tl), F32) for _ in range(3)]
                       + [pltpu.VMEM((PEER_TOPK, PEER_HEADS, tl), F32) for _ in range(2)],
        compiler_params=_params(("parallel",)), name="topk",
    )(s1, s2, s2h)


_ROWS = 32
_PACK = 16
_MM_ROWS = 512


def _experts_kernel(h2t_ref, u_ref, vt_ref, rank_ref, cnt_ref, p1_ref, p2_ref, x1_ref, gfin_ref,
                    out_ref, acc_ref, at_ref, ct_ref, cb_ref, pb_ref, *, gi, ne, nblk):
    step = pl.program_id(0)
    tm = h2t_ref.shape[1]
    blk_c = (step - 2) % ne

    @pl.when(step == 0)
    def _():
        at_ref[...] = jnp.zeros_like(at_ref)
        ct_ref[...] = jnp.zeros_like(ct_ref)

    @pl.when(jnp.logical_or(step == 0, blk_c == 0))
    def _():
        acc_ref[...] = jnp.zeros_like(acc_ref)

    gate_valid = jnp.logical_and(step >= 1, step <= nblk)
    jb = jnp.clip(step - 1, 0, nblk - 1) % ne

    def stages(slot):
        other = 1 - slot

        def value_rows(r0):
            rows = pl.ds(r0, _MM_ROWS)
            acc_ref[rows, :] += _dot(vt_ref[rows, :], ct_ref[slot])

        def preact_rows(r0):
            rows = pl.ds(r0, _MM_ROWS)
            at_ref[slot, rows, :] = _dot(u_ref[rows, :], h2t_ref[...])

        def column_rows(c0):
            ci = c0 // LANES
            ln = pl.ds(c0, LANES)
            for g in range(gi):
                cnt = jnp.where(gate_valid, cnt_ref[jb * gi + g, :, ln], 0.0)
                p1 = p1_ref[jb * gi + g, :, ln]
                for hd in range(PEER_HEADS):
                    cb_ref[ci, g * PEER_HEADS + hd] = jnp.broadcast_to(cnt[hd:hd + 1], (_PACK, LANES)).astype(BF16)
                    pb_ref[ci, g * PEER_HEADS + hd] = jnp.broadcast_to(p1[hd:hd + 1], (_PACK, LANES)).astype(BF16)

        def gate_chunk(c0, r0):
            if r0 == 0:
                column_rows(c0)
            ci = c0 // LANES
            ln = pl.ds(c0, LANES)
            subs = range(r0, r0 + _ROWS, _PACK)
            wgt = [[None] * len(subs) for _ in range(gi)]
            for hd in range(PEER_HEADS):
                rank = [rank_ref[hd, pl.ds(r, _PACK), ln].astype(BF16) for r in subs]
                p2 = [p2_ref[hd, pl.ds(r, _PACK), ln].astype(BF16) for r in subs]
                for g in range(gi):
                    cnt = cb_ref[ci, g * PEER_HEADS + hd]
                    p1 = pb_ref[ci, g * PEER_HEADS + hd]
                    for k in range(len(subs)):
                        term = jnp.where(rank[k] < cnt, p2[k], jnp.zeros_like(p2[k])) * p1
                        wgt[g][k] = term if wgt[g][k] is None else wgt[g][k] + term
            for g in range(gi):
                for k, r in enumerate(subs):
                    er = pl.ds(g * PEER_NKEYS + r, _PACK)
                    a = at_ref[other, er, ln]
                    act = a * (1.0 + lax.erf(a * (2.0 ** -0.5)))
                    ct_ref[other, er, ln] = act.astype(BF16) * wgt[g][k]

        mxu_tasks = ([functools.partial(value_rows, r0) for r0 in range(0, acc_ref.shape[0], _MM_ROWS)]
                     + [functools.partial(preact_rows, r0) for r0 in range(0, u_ref.shape[0], _MM_ROWS)])
        vpu_tasks = [functools.partial(gate_chunk, c0, r0)
                     for c0 in range(0, tm, LANES) for r0 in range(0, PEER_NKEYS, _ROWS)]
        done = 0
        for n, task in enumerate(vpu_tasks):
            want = ((n + 1) * len(mxu_tasks)) // len(vpu_tasks)
            while done < want:
                mxu_tasks[done]()
                done += 1
            task()

    @pl.when(step % 2 == 0)
    def _():
        stages(0)

    @pl.when(step % 2 == 1)
    def _():
        stages(1)

    @pl.when(jnp.logical_and(step >= 2, blk_c == ne - 1))
    def _():
        x = x1_ref[...] + acc_ref[...].T
        out_ref[...] = _rms(x, gfin_ref[...])


def _experts(h2t, u, vt, rank, cnt, p1, p2, x1, gfin, tm, te):
    T, D = x1.shape
    E = u.shape[0]
    gi = te // PEER_NKEYS
    ne = E // te
    nblk = (T // tm) * ne
    pair = lambda s, lag: jnp.clip(s - lag, 0, nblk - 1)
    tok = lambda n: pl.BlockSpec((tm, n), lambda s: (pair(s, 2) // ne, 0))
    lanes = lambda a: pl.BlockSpec((a.shape[0], a.shape[1], tm), lambda s: (0, 0, pair(s, 1) // ne))
    row_tables = pltpu.VMEM((tm // LANES, gi * PEER_HEADS, _PACK, LANES), BF16)
    return pl.pallas_call(
        functools.partial(_experts_kernel, gi=gi, ne=ne, nblk=nblk),
        grid=(nblk + 2,),
        in_specs=[pl.BlockSpec((D, tm), lambda s: (0, pair(s, 0) // ne)),
                  pl.BlockSpec((te, D), lambda s: (pair(s, 0) % ne, 0)),
                  pl.BlockSpec((D, te), lambda s: (0, pair(s, 2) % ne)),
                  lanes(rank), lanes(cnt), lanes(p1), lanes(p2), tok(D), _full(gfin.shape)],
        out_specs=tok(D),
        out_shape=jax.ShapeDtypeStruct((T, D), F32),
        scratch_shapes=[pltpu.VMEM((D, tm), F32), pltpu.VMEM((2, te, tm), F32), pltpu.VMEM((2, te, tm), BF16),
                        row_tables, row_tables],
        compiler_params=_params(("arbitrary",)), name="experts",
    )(h2t, u, vt, rank, cnt, p1, p2, x1, gfin)


def _rot_half_cols(w):
    half = w.shape[-1] // 2
    return jnp.concatenate([-w[..., half:], w[..., :half]], axis=-1)


def _prep_weights(g_mix, w_in, g_q_lat, w_qb, g_kv_lat, w_kvb, w_a2, b_a2, g_gla, w_branch_a, w_branch_b,
                  w_out, g_ffn, w_peer_q, peer_sub_keys):
    D = w_in.shape[0]
    sizes = (MLA_Q_RANK, MLA_KV_RANK, MLA_ROPE, GLA_HEADS * GLA_DK, GLA_HEADS * GLA_DK, GLA_HEADS * GLA_DV,
             GLA_GATE_RANK, GLA_HEADS * GLA_DV, 2 * D)
    pts = [int(s) for s in np.cumsum(sizes)[:-1]]
    w_ql, w_kvl, w_kr, w_gq, w_gk, w_gv, w_glr, w_go, w_br = jnp.split(w_in, pts, axis=1)
    H = MLA_HEADS
    zpad = lambda *shape: jnp.zeros(shape, F32)
    tail = HEAD_LANES - MLA_NOPE - MLA_ROPE

    def rope_block(wr):
        lead = wr.shape[:-1]
        return jnp.concatenate([zpad(*lead, MLA_NOPE), wr, zpad(*lead, tail)], -1)

    wq3 = w_qb.reshape(MLA_Q_RANK, H, MLA_NOPE + MLA_ROPE)
    q_rope = wq3[..., MLA_NOPE:]
    q0 = jnp.concatenate([wq3[..., :MLA_NOPE], q_rope, zpad(MLA_Q_RANK, H, tail)], -1)
    q1 = rope_block(_rot_half_cols(q_rope))
    wkv3 = w_kvb.reshape(MLA_KV_RANK, H, MLA_NOPE + MLA_V)
    wk = jnp.concatenate([wkv3[..., :MLA_NOPE], zpad(MLA_KV_RANK, H, HEAD_LANES - MLA_NOPE)], -1)
    wv = jnp.concatenate([wkv3[..., MLA_NOPE:], zpad(MLA_KV_RANK, H, HEAD_LANES - MLA_V)], -1)
    wa3 = w_branch_a.reshape(H, MLA_V, D)
    wa = jnp.concatenate([wa3, zpad(H, HEAD_LANES - MLA_V, D)], axis=1).reshape(H * HEAD_LANES, D)

    PH = PEER_HEADS
    wpq = w_peer_q.reshape(D, PH, 2, PEER_HALF).transpose(0, 2, 1, 3).reshape(D, 2 * PH * PEER_HALF)
    eye = jnp.eye(PH, dtype=F32)

    def keys_by_key_head(sk):
        return jnp.einsum('hg,hnd->nhgd', eye, sk).reshape(PEER_NKEYS * PH, PH * PEER_HALF)

    return {
        'g_mix': g_mix.reshape(1, D),
        'w_lat': jnp.concatenate([w_ql, w_kvl], 1).astype(BF16),
        'w_kr': jnp.concatenate([rope_block(w_kr), rope_block(_rot_half_cols(w_kr))], 1).astype(BF16),
        'w_g': jnp.concatenate([w_gq, w_gv, w_go], 1).astype(BF16),
        'w_gt': jnp.concatenate([w_gk, w_glr], 1).T.astype(BF16),
        'w_br': w_br.astype(BF16),
        'g_q_lat': g_q_lat.reshape(1, -1), 'g_kv_lat': g_kv_lat.reshape(1, -1),
        'w_q': q0.reshape(MLA_Q_RANK, H * HEAD_LANES).astype(BF16),
        'w_qr': q1.reshape(MLA_Q_RANK, H * HEAD_LANES).astype(BF16),
        'w_k': wk.reshape(MLA_KV_RANK, H * HEAD_LANES).astype(BF16),
        'w_v': wv.reshape(MLA_KV_RANK, H * HEAD_LANES).astype(BF16),
        'w_a2t': w_a2.T.astype(BF16), 'b_a2col': b_a2.reshape(-1, 1), 'g_gla': g_gla.reshape(1, -1),
        'w_a': wa.astype(BF16), 'w_b': w_branch_b.astype(BF16), 'w_out': w_out.astype(BF16),
        'g_ffn': g_ffn.reshape(1, D), 'w_pq': wpq.astype(BF16),
        'keys1': keys_by_key_head(peer_sub_keys[:, 0]).astype(BF16),
        'keys2': keys_by_key_head(peer_sub_keys[:, 1]).astype(BF16),
        'subk2': peer_sub_keys[:, 1].astype(BF16),
    }


def _rope_freq_lanes():
    half = MLA_ROPE // 2
    inv = ROPE_THETA ** (-np.arange(half, dtype=np.float32) / half)
    lanes = np.zeros((1, HEAD_LANES), np.float32)
    lanes[0, MLA_NOPE:MLA_NOPE + half] = inv
    lanes[0, MLA_NOPE + half:MLA_NOPE + 2 * half] = inv
    return jnp.asarray(lanes)


def _block(x, positions, w, peer_u, peer_v, g_final, *, tm_proj, tq, nsub, tb, tm_merge, tl, tm_exp, te):
    B, S, D = x.shape
    T = B * S
    x2 = x.reshape(T, D)
    posf = positions.reshape(T, 1).astype(F32)
    q, k, v, gq, gkt, gv, glrt, gout, br = _proj(x2, posf, _rope_freq_lanes(), w, tm_proj)
    ya = _attn(q, k, v, B, S, tq, nsub)
    yb = _gla(gq, gkt, gv, glrt, gout, w, B, S, tb)
    x1, h2t, s1, s2, s2h = _merge(x2, ya, yb, br, w, tm_merge)
    s1 = s1.reshape(PEER_NKEYS, PEER_HEADS, T)
    s2 = s2.reshape(PEER_NKEYS, PEER_HEADS, T)
    cnt, p1, rank, p2 = _topk(s1, s2, s2h, tl)
    out = _experts(h2t, peer_u.astype(BF16), peer_v.T.astype(BF16), rank, cnt, p1, p2, x1,
                   g_final.reshape(1, D), tm_exp, te)
    return out.reshape(B, S, D)


def kernel(x, positions, g_mix, w_in, g_q_lat, w_qb, g_kv_lat, w_kvb, w_a2, b_a2, g_gla, w_branch_a, w_branch_b, w_out, g_ffn, w_peer_q, peer_sub_keys, peer_u, peer_v, g_final):
    assert g_mix.shape[0] == 1, "single-layer block"
    w = _prep_weights(g_mix[0], w_in[0], g_q_lat[0], w_qb[0], g_kv_lat[0], w_kvb[0], w_a2[0], b_a2[0], g_gla[0],
                      w_branch_a[0], w_branch_b[0], w_out[0], g_ffn[0], w_peer_q[0], peer_sub_keys[0])
    return _block(x, positions, w, peer_u[0], peer_v[0], g_final,
                  tm_proj=512, tq=1024, nsub=2, tb=512, tm_merge=512, tl=256, tm_exp=512, te=1024)
```

```python
import functools
import math

import numpy as np
import jax
import jax.numpy as jnp
from jax import lax
from jax.experimental import pallas as pl
from jax.experimental.pallas import tpu as pltpu

F32 = jnp.float32
BF16 = jnp.bfloat16

EPS = 1e-6
ROPE_THETA = 10000.0
CHUNK = 64

MLA_HEADS = 8
MLA_Q_RANK = 384
MLA_KV_RANK = 256
MLA_NOPE = 64
MLA_ROPE = 32
MLA_V = 64
HEAD_LANES = 128

GLA_HEADS = 4
GLA_DK = 64
GLA_DV = 128
GLA_GATE_RANK = 16
GLA_TAU = 16.0

PEER_HEADS = 8
PEER_NKEYS = 128
PEER_HALF = 128
PEER_TOPK = 16

LANES = 128
VMEM_LIMIT_BYTES = 56 * 1024 * 1024

_NT = (((1,), (1,)), ((), ()))


def _dot(a, b):
    return jnp.dot(a, b, preferred_element_type=F32)


def _dot_nt(a, b):
    return lax.dot_general(a, b, _NT, preferred_element_type=F32)


def _rms(x, g):
    return x * lax.rsqrt(jnp.mean(x * x, axis=-1, keepdims=True) + EPS) * g


def _params(sem):
    return pltpu.CompilerParams(dimension_semantics=sem, vmem_limit_bytes=VMEM_LIMIT_BYTES)


def _full(shape):
    nd = len(shape)
    return pl.BlockSpec(shape, lambda *_: (0,) * nd, pipeline_mode=pl.Buffered(1))


def _proj_kernel(x_ref, pos_ref, freq_ref, gmix_ref, wlat_ref, wkr_ref, wg_ref, wgt_ref, wbr_ref,
                 gql_ref, wq_ref, wqr_ref, gkv_ref, wk_ref, wv_ref,
                 q_out, k_out, v_out, gq_out, gkt_out, gv_out, glrt_out, gout_out, br_out, *, q_scale):
    h = _rms(x_ref[...], gmix_ref[...]).astype(BF16)
    ang = pos_ref[...] * freq_ref[...]
    cos = jnp.cos(ang)
    sin = jnp.sin(ang)

    lat = _dot(h, wlat_ref[...])
    qn = _rms(lat[:, :MLA_Q_RANK], gql_ref[...]).astype(BF16)
    kvn = _rms(lat[:, MLA_Q_RANK:], gkv_ref[...]).astype(BF16)
    q0 = _dot(qn, wq_ref[...])
    q1 = _dot(qn, wqr_ref[...])
    kr = _dot(h, wkr_ref[...])
    kpe = kr[:, :HEAD_LANES] * cos + kr[:, HEAD_LANES:] * sin
    kk = _dot(kvn, wk_ref[...])
    for hd in range(MLA_HEADS):
        sl = slice(hd * HEAD_LANES, (hd + 1) * HEAD_LANES)
        q_out[:, sl] = ((q0[:, sl] * cos + q1[:, sl] * sin) * q_scale).astype(BF16)
        k_out[:, sl] = (kk[:, sl] + kpe).astype(BF16)
    ones_lane = lax.broadcasted_iota(jnp.int32, (1, MLA_HEADS * HEAD_LANES), 1) % HEAD_LANES == MLA_V
    v_out[...] = (_dot(kvn, wv_ref[...]) + ones_lane.astype(F32)).astype(BF16)

    nqk = GLA_HEADS * GLA_DK
    nv = GLA_HEADS * GLA_DV
    g = _dot(h, wg_ref[...])
    gq_out[...] = (g[:, :nqk] * (GLA_DK ** -0.5)).astype(BF16)
    gv_out[...] = g[:, nqk:nqk + nv].astype(BF16)
    gout_out[...] = g[:, nqk + nv:]
    gt = _dot_nt(wgt_ref[...], h)
    gkt_out[...] = gt[:nqk]
    glrt_out[...] = gt[nqk:].astype(BF16)
    br_out[...] = _dot(h, wbr_ref[...])


def _proj(x2, posf, freq, w, tm):
    T, D = x2.shape
    nqk = GLA_HEADS * GLA_DK
    nv = GLA_HEADS * GLA_DV
    hl = MLA_HEADS * HEAD_LANES
    row = lambda n: pl.BlockSpec((tm, n), lambda i: (i, 0))
    col = lambda n: pl.BlockSpec((n, tm), lambda i: (0, i))
    ins = [x2, posf, freq, w['g_mix'], w['w_lat'], w['w_kr'], w['w_g'], w['w_gt'], w['w_br'],
           w['g_q_lat'], w['w_q'], w['w_qr'], w['g_kv_lat'], w['w_k'], w['w_v']]
    in_specs = [row(D), row(1)] + [_full(a.shape) for a in ins[2:]]
    out_shape = [
        jax.ShapeDtypeStruct((T, hl), BF16), jax.ShapeDtypeStruct((T, hl), BF16),
        jax.ShapeDtypeStruct((T, hl), BF16),
        jax.ShapeDtypeStruct((T, nqk), BF16), jax.ShapeDtypeStruct((nqk, T), F32),
        jax.ShapeDtypeStruct((T, nv), BF16), jax.ShapeDtypeStruct((GLA_GATE_RANK, T), BF16),
        jax.ShapeDtypeStruct((T, nv), F32), jax.ShapeDtypeStruct((T, 2 * D), F32),
    ]
    out_specs = [row(hl), row(hl), row(hl), row(nqk), col(nqk), row(nv), col(GLA_GATE_RANK),
                 row(nv), row(2 * D)]
    q_scale = (MLA_NOPE + MLA_ROPE) ** -0.5 * math.log2(math.e)
    return pl.pallas_call(
        functools.partial(_proj_kernel, q_scale=q_scale),
        grid=(T // tm,), in_specs=in_specs, out_specs=out_specs, out_shape=out_shape,
        compiler_params=_params(("parallel",)), name="proj",
    )(*ins)


def _attn_kernel(q_ref, k_ref, v_ref, o_ref, sa_ref, sb_ref, *, tq, nsub):
    i = pl.program_id(2)
    qs = [q_ref[pl.ds(s * tq, tq), :] for s in range(nsub)]

    def rows(ref, j):
        return ref[pl.ds(pl.multiple_of(j * tq, tq), tq), :]

    def scores(j, dst, subs):
        kj = rows(k_ref, j)
        for s in subs:
            dst[s] = _dot_nt(qs[s], kj)

    def update(carry, s, vj, masked):
        m, acc = carry
        if masked:
            qc = lax.broadcasted_iota(jnp.int32, (tq, tq), 0) // CHUNK
            kc = lax.broadcasted_iota(jnp.int32, (tq, tq), 1) // CHUNK
            s = jnp.where(kc <= qc, s, -1e30)
        m_new = jnp.maximum(m, jnp.max(s, axis=-1, keepdims=True))
        p = jnp.exp2(s - m_new)
        acc = jnp.exp2(m - m_new) * acc + _dot(p.astype(BF16), vj)
        return m_new, acc

    def phase(j, carry, cur, nxt):
        scores(j + 1, nxt, range(nsub))
        vj = rows(v_ref, j)
        return tuple(update(carry[s], cur[s], vj, False) for s in range(nsub))

    def body(jj, carry):
        carry = phase(2 * jj, carry, sa_ref, sb_ref)
        return phase(2 * jj + 1, carry, sb_ref, sa_ref)

    init = tuple((jnp.full((tq, 1), -1e30, F32), jnp.zeros((tq, HEAD_LANES), F32)) for _ in range(nsub))
    scores(0, sa_ref, range(nsub))
    carry = list(lax.fori_loop(0, i, body, init))
    scores(2 * i + 1, sb_ref, [1])
    v0 = rows(v_ref, 2 * i)
    carry[0] = update(carry[0], sa_ref[0], v0, True)
    carry[1] = update(carry[1], sa_ref[1], v0, False)
    carry[1] = update(carry[1], sb_ref[1], rows(v_ref, 2 * i + 1), True)
    lane = lax.broadcasted_iota(jnp.int32, (tq, HEAD_LANES), 1)
    for s in range(nsub):
        acc = carry[s][1]
        o = acc / acc[:, MLA_V:MLA_V + 1]
        o_ref[pl.ds(s * tq, tq), :] = jnp.where(lane < MLA_V, o, 0.0).astype(BF16)


def _attn(q, k, v, B, S, tq, nsub):
    assert nsub == 2, "the score double-buffering is written for two query sub-tiles per step"
    T = q.shape[0]
    nq = S // (tq * nsub)
    qspec = pl.BlockSpec((tq * nsub, HEAD_LANES), lambda b, h, i: (b * nq + i, h))
    kvspec = pl.BlockSpec((S, HEAD_LANES), lambda b, h, i: (b, h))
    return pl.pallas_call(
        functools.partial(_attn_kernel, tq=tq, nsub=nsub),
        grid=(B, MLA_HEADS, nq), in_specs=[qspec, kvspec, kvspec], out_specs=qspec,
        out_shape=jax.ShapeDtypeStruct((T, MLA_HEADS * HEAD_LANES), BF16),
        scratch_shapes=[pltpu.VMEM((nsub, tq, tq), F32), pltpu.VMEM((nsub, tq, tq), F32)],
        compiler_params=_params(("parallel", "parallel", "arbitrary")), name="attn",
    )(q, k, v)


def _split3(x):
    hi = x.astype(BF16)
    r = x - hi.astype(F32)
    mid = r.astype(BF16)
    lo = (r - mid.astype(F32)).astype(BF16)
    return hi, mid, lo


def _gla_kernel(gq_ref, gkt_ref, gv_ref, glrt_ref, gout_ref, wat_ref, bcol_ref, ggn_ref, sufm_ref, chm_ref,
                y_ref, st_ref, *, tb):
    @pl.when(pl.program_id(1) == 0)
    def _():
        st_ref[...] = jnp.zeros_like(st_ref)

    z = _dot(wat_ref[...], glrt_ref[...]) + bcol_ref[...]
    la = jax.nn.log_sigmoid(z) * (1.0 / GLA_TAU)
    parts = _split3(la)
    sufm = sufm_ref[...]
    chm = chm_ref[...]
    suf = _dot(parts[0], sufm) + _dot(parts[1], sufm) + _dot(parts[2], sufm)
    tot = _dot(parts[0], chm) + _dot(parts[1], chm) + _dot(parts[2], chm)
    kdec = (gkt_ref[...] * jnp.exp(suf)).astype(BF16)
    dec = jnp.exp(tot)
    gq = gq_ref[...]
    gv = gv_ref[...]
    for c in range(tb // CHUNK):
        fr = slice(c * CHUNK, (c + 1) * CHUNK)
        for hd in range(GLA_HEADS):
            kr = slice(hd * GLA_DK, (hd + 1) * GLA_DK)
            vr = slice(hd * GLA_DV, (hd + 1) * GLA_DV)
            ds = _dot(kdec[kr, fr], gv[fr, vr])
            st = dec[kr, c:c + 1] * st_ref[hd] + ds
            st_ref[hd] = st
            o = _dot(gq[fr, kr], st.astype(BF16))
            o = o * lax.rsqrt(jnp.mean(o * o, axis=-1, keepdims=True) + EPS) * ggn_ref[:, vr]
            y_ref[fr, vr] = (o * jax.nn.silu(gout_ref[fr, vr])).astype(BF16)


def _gla(gq, gkt, gv, glrt, gout, w, B, S, tb):
    T = gq.shape[0]
    nb = S // tb
    nqk = GLA_HEADS * GLA_DK
    nv = GLA_HEADS * GLA_DV
    row = lambda n: pl.BlockSpec((tb, n), lambda b, j: (b * nb + j, 0))
    col = lambda n: pl.BlockSpec((n, tb), lambda b, j: (0, b * nb + j))
    fr = np.arange(tb)
    same = (fr[:, None] // CHUNK) == (fr[None, :] // CHUNK)
    sufm = jnp.asarray(same & (fr[:, None] > fr[None, :]), BF16)
    chm = jnp.asarray((fr[:, None] // CHUNK) == np.arange(LANES)[None, :], BF16)
    consts = [w['w_a2t'], w['b_a2col'], w['g_gla'], sufm, chm]
    return pl.pallas_call(
        functools.partial(_gla_kernel, tb=tb),
        grid=(B, nb),
        in_specs=[row(nqk), col(nqk), row(nv), col(GLA_GATE_RANK), row(nv)] + [_full(a.shape) for a in consts],
        out_specs=row(nv),
        out_shape=jax.ShapeDtypeStruct((T, nv), BF16),
        scratch_shapes=[pltpu.VMEM((GLA_HEADS, GLA_DK, GLA_DV), F32)],
        compiler_params=_params(("parallel", "arbitrary")), name="gla",
    )(gq, gkt, gv, glrt, gout, *consts)


def _merge_kernel(x_ref, ya_ref, yb_ref, br_ref, wa_ref, wb_ref, wo_ref, gffn_ref, wpq_ref, subk1_ref, subk2_ref,
                  x1_out, h2t_out, s1_out, s2_out, s2h_out):
    D = x_ref.shape[1]
    gates = jax.nn.sigmoid(br_ref[...])
    merged = gates[:, :D] * _dot(ya_ref[...], wa_ref[...]) + gates[:, D:] * _dot(yb_ref[...], wb_ref[...])
    x1 = x_ref[...] + _dot(merged.astype(BF16), wo_ref[...])
    x1_out[...] = x1
    h2f = _rms(x1, gffn_ref[...])
    h2 = h2f.astype(BF16)
    h2t_out[...] = h2f.T.astype(BF16)
    pq = _dot(h2, wpq_ref[...]).astype(BF16)
    nh = PEER_HEADS * PEER_HALF

    def head_scores(sk_ref, lo):
        return [_dot_nt(sk_ref[hd], pq[:, lo + hd * PEER_HALF:lo + (hd + 1) * PEER_HALF]) for hd in range(PEER_HEADS)]

    def key_head_rows(per_head):
        swapped = pltpu.einshape("hnt->nht", jnp.stack(per_head, axis=0))
        return swapped.reshape(PEER_NKEYS * PEER_HEADS, swapped.shape[-1])

    s1h = head_scores(subk1_ref, 0)
    s2h = head_scores(subk2_ref, nh)
    s1_out[...] = key_head_rows(s1h)
    s2_out[...] = key_head_rows(s2h)
    for hd in range(PEER_HEADS):
        s2h_out[hd] = s2h[hd]


def _merge(x2, ya, yb, br, w, tm):
    T, D = x2.shape
    row = lambda n: pl.BlockSpec((tm, n), lambda i: (i, 0))
    col = lambda n: pl.BlockSpec((n, tm), lambda i: (0, i))
    nkh = PEER_NKEYS * PEER_HEADS
    consts = [w['w_a'], w['w_b'], w['w_out'], w['g_ffn'], w['w_pq'], w['subk1'], w['subk2']]
    return pl.pallas_call(
        _merge_kernel,
        grid=(T // tm,),
        in_specs=[row(D), row(ya.shape[1]), row(yb.shape[1]), row(2 * D)] + [_full(a.shape) for a in consts],
        out_specs=[row(D), col(D), col(nkh), col(nkh),
                   pl.BlockSpec((PEER_HEADS, PEER_NKEYS, tm), lambda i: (0, 0, i))],
        out_shape=[jax.ShapeDtypeStruct((T, D), F32), jax.ShapeDtypeStruct((D, T), BF16),
                   jax.ShapeDtypeStruct((nkh, T), F32), jax.ShapeDtypeStruct((nkh, T), F32),
                   jax.ShapeDtypeStruct((PEER_HEADS, PEER_NKEYS, T), F32)],
        compiler_params=_params(("parallel",)), name="merge",
    )(x2, ya, yb, br, *consts)


_NSORT = PEER_TOPK + 1
_CAND = [(i, j) for i in range(_NSORT) for j in range(_NSORT) if (i + 1) * (j + 1) <= _NSORT]


def _insert_sorted(ms, x):
    out = []
    for m in ms:
        out.append(jnp.maximum(m, x))
        x = jnp.minimum(m, x)
    return tuple(out)


def _bitonic_merge(xs):
    n = len(xs)
    if n == 1:
        return xs
    half = n // 2
    hi = [jnp.maximum(xs[i], xs[i + half]) for i in range(half)]
    lo = [jnp.minimum(xs[i], xs[i + half]) for i in range(half)]
    return _bitonic_merge(hi) + _bitonic_merge(lo)


def _bitonic_sort(xs):
    if len(xs) == 1:
        return xs
    half = len(xs) // 2
    return _bitonic_merge(_bitonic_sort(xs[:half]) + _bitonic_sort(xs[half:])[::-1])


def _merge_top(t1, e1, t2, e2):
    k = len(t1)
    hi = [jnp.maximum(t1[i], t2[k - 1 - i]) for i in range(k)]
    lo = [jnp.minimum(t1[i], t2[k - 1 - i]) for i in range(k)]
    return _bitonic_merge(hi), functools.reduce(jnp.maximum, lo + [e1, e2])


def _count_sorted(v, x, strict):
    above = (lambda a: a > x) if strict else (lambda a: a >= x)
    c = [above(v[4 * i + 3]) for i in range(4)]
    pick = lambda j: jnp.where(c[2], v[12 + j], jnp.where(c[1], v[8 + j], jnp.where(c[0], v[4 + j], v[j])))
    fine = jnp.where(above(pick(2)), 3.0, jnp.where(above(pick(1)), 2.0, jnp.where(above(pick(0)), 1.0, 0.0)))
    base = jnp.where(c[2], 12.0, jnp.where(c[1], 8.0, jnp.where(c[0], 4.0, 0.0)))
    return jnp.where(c[3], 16.0, base + fine)


_KEY_ROWS = 16
_RANK_ROWS = 64


def _topk_kernel(s1_ref, s2_ref, s2h_ref, cnt_out, p1_out, rank_out, p2_out,
                 thr_ref, top1_ref, zinv_ref, b_ref, mid_ref, *, tl):
    neg = jnp.full((PEER_HEADS, LANES), -jnp.inf, F32)

    def largest(s_ref, ln):
        def summary(lo, hi):
            if hi - lo == PEER_TOPK:
                return _bitonic_sort([s_ref[n, :, ln] for n in range(lo, hi)]), neg
            mid = (lo + hi) // 2
            return _merge_top(*summary(lo, mid), *summary(mid, hi))

        top, nxt = summary(0, PEER_NKEYS)
        return tuple(top) + (nxt,)

    for c0 in range(0, tl, LANES):
        ln = pl.ds(c0, LANES)
        a = largest(s1_ref, ln)
        b = largest(s2_ref, ln)
        cands = [a[i] + b[j] for (i, j) in _CAND]
        best = (neg,) * _NSORT
        for cd in cands:
            best = _insert_sorted(best, cd)
        thr = 0.5 * (best[PEER_TOPK - 1] + best[PEER_TOPK])
        top = a[0] + b[0]
        z = functools.reduce(lambda u, v: u + v, [jnp.where(cd >= thr, jnp.exp(cd - top), 0.0) for cd in cands])
        thr_ref[:, ln] = thr
        top1_ref[:, ln] = a[0]
        zinv_ref[:, ln] = 0.5 / z
        for kk in range(PEER_TOPK):
            b_ref[kk, :, ln] = b[kk]
            mid_ref[kk, :, ln] = 0.5 * (b[kk] + b[kk + 1])

    def count_rows(c, _):
        rows = pl.ds(pl.multiple_of(c * _KEY_ROWS, _KEY_ROWS), _KEY_ROWS)
        s1 = s1_ref[rows]
        tau = thr_ref[...][None] - s1
        cnt_out[rows] = _count_sorted([b_ref[kk][None] for kk in range(PEER_TOPK)], tau, strict=False)
        p1_out[rows] = jnp.exp(s1 - top1_ref[...][None])
        return 0

    lax.fori_loop(0, PEER_NKEYS // _KEY_ROWS, count_rows, 0)

    for hd in range(PEER_HEADS):
        def rank_rows(c, _, hd=hd):
            rows = pl.ds(pl.multiple_of(c * _RANK_ROWS, _RANK_ROWS), _RANK_ROWS)
            s2 = s2h_ref[hd, rows, :]
            mids = [mid_ref[kk, hd:hd + 1, :] for kk in range(PEER_TOPK)]
            rank_out[hd, rows, :] = _count_sorted(mids, s2, strict=True)
            p2_out[hd, rows, :] = jnp.exp(s2 - b_ref[0, hd:hd + 1, :]) * zinv_ref[hd:hd + 1, :]
            return 0

        lax.fori_loop(0, PEER_NKEYS // _RANK_ROWS, rank_rows, 0)


def _topk(s1, s2, s2h, tl):
    T = s2h.shape[2]
    nh_spec = pl.BlockSpec((PEER_NKEYS, PEER_HEADS, tl), lambda i: (0, 0, i))
    hn_spec = pl.BlockSpec((PEER_HEADS, PEER_NKEYS, tl), lambda i: (0, 0, i))
    nh_shape = jax.ShapeDtypeStruct((PEER_NKEYS, PEER_HEADS, T), F32)
    hn_shape = jax.ShapeDtypeStruct((PEER_HEADS, PEER_NKEYS, T), F32)
    return pl.pallas_call(
        functools.partial(_topk_kernel, tl=tl),
        grid=(T // tl,),
        in_specs=[nh_spec, nh_spec, hn_spec],
        out_specs=[nh_spec, nh_spec, hn_spec, hn_spec], out_shape=[nh_shape, nh_shape, hn_shape, hn_shape],
        scratch_shapes=[pltpu.VMEM((PEER_HEADS, tl), F32) for _ in range(3)]
                       + [pltpu.VMEM((PEER_TOPK, PEER_HEADS, tl), F32) for _ in range(2)],
        compiler_params=_params(("parallel",)), name="topk",
    )(s1, s2, s2h)


_ROWS = 32
_PACK = 16
_MM_ROWS = 512


def _experts_kernel(h2t_ref, u_ref, vt_ref, rank_ref, cnt_ref, p1_ref, p2_ref, x1_ref, gfin_ref,
                    out_ref, acc_ref, at_ref, ct_ref, cb_ref, pb_ref, *, gi, ne, nblk):
    step = pl.program_id(0)
    tm = h2t_ref.shape[1]
    blk_c = (step - 2) % ne

    @pl.when(step == 0)
    def _():
        at_ref[...] = jnp.zeros_like(at_ref)
        ct_ref[...] = jnp.zeros_like(ct_ref)

    @pl.when(jnp.logical_or(step == 0, blk_c == 0))
    def _():
        acc_ref[...] = jnp.zeros_like(acc_ref)

    gate_valid = jnp.logical_and(step >= 1, step <= nblk)
    jb = jnp.clip(step - 1, 0, nblk - 1) % ne

    def stages(slot):
        other = 1 - slot

        def value_rows(r0):
            rows = pl.ds(r0, _MM_ROWS)
            acc_ref[rows, :] += _dot(vt_ref[rows, :], ct_ref[slot])

        def preact_rows(r0):
            rows = pl.ds(r0, _MM_ROWS)
            at_ref[slot, rows, :] = _dot(u_ref[rows, :], h2t_ref[...])

        def column_rows(c0):
            ci = c0 // LANES
            ln = pl.ds(c0, LANES)
            for g in range(gi):
                cnt = jnp.where(gate_valid, cnt_ref[jb * gi + g, :, ln], 0.0)
                p1 = p1_ref[jb * gi + g, :, ln]
                for hd in range(PEER_HEADS):
                    cb_ref[ci, g * PEER_HEADS + hd] = jnp.broadcast_to(cnt[hd:hd + 1], (_PACK, LANES)).astype(BF16)
                    pb_ref[ci, g * PEER_HEADS + hd] = jnp.broadcast_to(p1[hd:hd + 1], (_PACK, LANES)).astype(BF16)

        def gate_chunk(c0, r0):
            if r0 == 0:
                column_rows(c0)
            ci = c0 // LANES
            ln = pl.ds(c0, LANES)
            subs = range(r0, r0 + _ROWS, _PACK)
            wgt = [[None] * len(subs) for _ in range(gi)]
            for hd in range(PEER_HEADS):
                rank = [rank_ref[hd, pl.ds(r, _PACK), ln].astype(BF16) for r in subs]
                p2 = [p2_ref[hd, pl.ds(r, _PACK), ln].astype(BF16) for r in subs]
                for g in range(gi):
                    cnt = cb_ref[ci, g * PEER_HEADS + hd]
                    p1 = pb_ref[ci, g * PEER_HEADS + hd]
                    for k in range(len(subs)):
                        term = jnp.where(rank[k] < cnt, p2[k], jnp.zeros_like(p2[k])) * p1
                        wgt[g][k] = term if wgt[g][k] is None else wgt[g][k] + term
            for g in range(gi):
                for k, r in enumerate(subs):
                    er = pl.ds(g * PEER_NKEYS + r, _PACK)
                    a = at_ref[other, er, ln]
                    act = a * (1.0 + lax.erf(a * (2.0 ** -0.5)))
                    ct_ref[other, er, ln] = act.astype(BF16) * wgt[g][k]

        mxu_tasks = ([functools.partial(value_rows, r0) for r0 in range(0, acc_ref.shape[0], _MM_ROWS)]
                     + [functools.partial(preact_rows, r0) for r0 in range(0, u_ref.shape[0], _MM_ROWS)])
        vpu_tasks = [functools.partial(gate_chunk, c0, r0)
                     for c0 in range(0, tm, LANES) for r0 in range(0, PEER_NKEYS, _ROWS)]
        done = 0
        for n, task in enumerate(vpu_tasks):
            want = ((n + 1) * len(mxu_tasks)) // len(vpu_tasks)
            while done < want:
                mxu_tasks[done]()
                done += 1
            task()

    @pl.when(step % 2 == 0)
    def _():
        stages(0)

    @pl.when(step % 2 == 1)
    def _():
        stages(1)

    @pl.when(jnp.logical_and(step >= 2, blk_c == ne - 1))
    def _():
        x = x1_ref[...] + acc_ref[...].T
        out_ref[...] = _rms(x, gfin_ref[...])


def _experts(h2t, u, vt, rank, cnt, p1, p2, x1, gfin, tm, te):
    T, D = x1.shape
    E = u.shape[0]
    gi = te // PEER_NKEYS
    ne = E // te
    nblk = (T // tm) * ne
    pair = lambda s, lag: jnp.clip(s - lag, 0, nblk - 1)
    tok = lambda n: pl.BlockSpec((tm, n), lambda s: (pair(s, 2) // ne, 0))
    lanes = lambda a: pl.BlockSpec((a.shape[0], a.shape[1], tm), lambda s: (0, 0, pair(s, 1) // ne))
    row_tables = pltpu.VMEM((tm // LANES, gi * PEER_HEADS, _PACK, LANES), BF16)
    return pl.pallas_call(
        functools.partial(_experts_kernel, gi=gi, ne=ne, nblk=nblk),
        grid=(nblk + 2,),
        in_specs=[pl.BlockSpec((D, tm), lambda s: (0, pair(s, 0) // ne)),
                  pl.BlockSpec((te, D), lambda s: (pair(s, 0) % ne, 0)),
                  pl.BlockSpec((D, te), lambda s: (0, pair(s, 2) % ne)),
                  lanes(rank), lanes(cnt), lanes(p1), lanes(p2), tok(D), _full(gfin.shape)],
        out_specs=tok(D),
        out_shape=jax.ShapeDtypeStruct((T, D), F32),
        scratch_shapes=[pltpu.VMEM((D, tm), F32), pltpu.VMEM((2, te, tm), F32), pltpu.VMEM((2, te, tm), BF16),
                        row_tables, row_tables],
        compiler_params=_params(("arbitrary",)), name="experts",
    )(h2t, u, vt, rank, cnt, p1, p2, x1, gfin)


def _rot_half_cols(w):
    half = w.shape[-1] // 2
    return jnp.concatenate([-w[..., half:], w[..., :half]], axis=-1)


def _prep_weights(g_mix, w_in, g_q_lat, w_qb, g_kv_lat, w_kvb, w_a2, b_a2, g_gla, w_branch_a, w_branch_b,
                  w_out, g_ffn, w_peer_q, peer_sub_keys):
    D = w_in.shape[0]
    sizes = (MLA_Q_RANK, MLA_KV_RANK, MLA_ROPE, GLA_HEADS * GLA_DK, GLA_HEADS * GLA_DK, GLA_HEADS * GLA_DV,
             GLA_GATE_RANK, GLA_HEADS * GLA_DV, 2 * D)
    pts = [int(s) for s in np.cumsum(sizes)[:-1]]
    w_ql, w_kvl, w_kr, w_gq, w_gk, w_gv, w_glr, w_go, w_br = jnp.split(w_in, pts, axis=1)
    H = MLA_HEADS
    zpad = lambda *shape: jnp.zeros(shape, F32)
    tail = HEAD_LANES - MLA_NOPE - MLA_ROPE

    def rope_block(wr):
        lead = wr.shape[:-1]
        return jnp.concatenate([zpad(*lead, MLA_NOPE), wr, zpad(*lead, tail)], -1)

    wq3 = w_qb.reshape(MLA_Q_RANK, H, MLA_NOPE + MLA_ROPE)
    q_rope = wq3[..., MLA_NOPE:]
    q0 = jnp.concatenate([wq3[..., :MLA_NOPE], q_rope, zpad(MLA_Q_RANK, H, tail)], -1)
    q1 = rope_block(_rot_half_cols(q_rope))
    wkv3 = w_kvb.reshape(MLA_KV_RANK, H, MLA_NOPE + MLA_V)
    wk = jnp.concatenate([wkv3[..., :MLA_NOPE], zpad(MLA_KV_RANK, H, HEAD_LANES - MLA_NOPE)], -1)
    wv = jnp.concatenate([wkv3[..., MLA_NOPE:], zpad(MLA_KV_RANK, H, HEAD_LANES - MLA_V)], -1)
    wa3 = w_branch_a.reshape(H, MLA_V, D)
    wa = jnp.concatenate([wa3, zpad(H, HEAD_LANES - MLA_V, D)], axis=1).reshape(H * HEAD_LANES, D)

    PH = PEER_HEADS
    wpq = w_peer_q.reshape(D, PH, 2, PEER_HALF).transpose(0, 2, 1, 3).reshape(D, 2 * PH * PEER_HALF)
    eye = jnp.eye(PH, dtype=F32)

    def keys_by_key_head(sk):
        return jnp.einsum('hg,hnd->nhgd', eye, sk).reshape(PEER_NKEYS * PH, PH * PEER_HALF)

    return {
        'g_mix': g_mix.reshape(1, D),
        'w_lat': jnp.concatenate([w_ql, w_kvl], 1).astype(BF16),
        'w_kr': jnp.concatenate([rope_block(w_kr), rope_block(_rot_half_cols(w_kr))], 1).astype(BF16),
        'w_g': jnp.concatenate([w_gq, w_gv, w_go], 1).astype(BF16),
        'w_gt': jnp.concatenate([w_gk, w_glr], 1).T.astype(BF16),
        'w_br': w_br.astype(BF16),
        'g_q_lat': g_q_lat.reshape(1, -1), 'g_kv_lat': g_kv_lat.reshape(1, -1),
        'w_q': q0.reshape(MLA_Q_RANK, H * HEAD_LANES).astype(BF16),
        'w_qr': q1.reshape(MLA_Q_RANK, H * HEAD_LANES).astype(BF16),
        'w_k': wk.reshape(MLA_KV_RANK, H * HEAD_LANES).astype(BF16),
        'w_v': wv.reshape(MLA_KV_RANK, H * HEAD_LANES).astype(BF16),
        'w_a2t': w_a2.T.astype(BF16), 'b_a2col': b_a2.reshape(-1, 1), 'g_gla': g_gla.reshape(1, -1),
        'w_a': wa.astype(BF16), 'w_b': w_branch_b.astype(BF16), 'w_out': w_out.astype(BF16),
        'g_ffn': g_ffn.reshape(1, D), 'w_pq': wpq.astype(BF16),
        'subk1': peer_sub_keys[:, 0].astype(BF16), 'subk2': peer_sub_keys[:, 1].astype(BF16),
    }


def _rope_freq_lanes():
    half = MLA_ROPE // 2
    inv = ROPE_THETA ** (-np.arange(half, dtype=np.float32) / half)
    lanes = np.zeros((1, HEAD_LANES), np.float32)
    lanes[0, MLA_NOPE:MLA_NOPE + half] = inv
    lanes[0, MLA_NOPE + half:MLA_NOPE + 2 * half] = inv
    return jnp.asarray(lanes)


def _block(x, positions, w, peer_u, peer_v, g_final, *, tm_proj, tq, nsub, tb, tm_merge, tl, tm_exp, te):
    B, S, D = x.shape
    T = B * S
    x2 = x.reshape(T, D)
    posf = positions.reshape(T, 1).astype(F32)
    q, k, v, gq, gkt, gv, glrt, gout, br = _proj(x2, posf, _rope_freq_lanes(), w, tm_proj)
    ya = _attn(q, k, v, B, S, tq, nsub)
    yb = _gla(gq, gkt, gv, glrt, gout, w, B, S, tb)
    x1, h2t, s1, s2, s2h = _merge(x2, ya, yb, br, w, tm_merge)
    s1 = s1.reshape(PEER_NKEYS, PEER_HEADS, T)
    s2 = s2.reshape(PEER_NKEYS, PEER_HEADS, T)
    cnt, p1, rank, p2 = _topk(s1, s2, s2h, tl)
    out = _experts(h2t, peer_u.astype(BF16), peer_v.T.astype(BF16), rank, cnt, p1, p2, x1,
                   g_final.reshape(1, D), tm_exp, te)
    return out.reshape(B, S, D)


def kernel(x, positions, g_mix, w_in, g_q_lat, w_qb, g_kv_lat, w_kvb, w_a2, b_a2, g_gla, w_branch_a, w_branch_b, w_out, g_ffn, w_peer_q, peer_sub_keys, peer_u, peer_v, g_final):
    assert g_mix.shape[0] == 1, "single-layer block"
    w = _prep_weights(g_mix[0], w_in[0], g_q_lat[0], w_qb[0], g_kv_lat[0], w_kvb[0], w_a2[0], b_a2[0], g_gla[0],
                      w_branch_a[0], w_branch_b[0], w_out[0], g_ffn[0], w_peer_q[0], peer_sub_keys[0])
    return _block(x, positions, w, peer_u[0], peer_v[0], g_final,
                  tm_proj=512, tq=1024, nsub=2, tb=512, tm_merge=512, tl=256, tm_exp=512, te=1024)
```
